```python
import math
import jax
import jax.numpy as jnp
from jax import lax
import numpy as np

D_MODEL = 1024
BATCH = 16
SEQ = 2048
DEPTH = 2
DEC_BATCH = 32
DEC_SEQ = 4
PAST_LEN = 16384
PAGE_SIZE = 128

HA = 4
DA = 128
DA_HALF = DA // 2
HB = 4
DB = 128
MOBA_BLOCK = 256
MOBA_TOPK = 3
MOBA_Q_CHUNK = 16
HC = 4
DK = 128
DV = 128
CONV_W = 4
GDN_CHUNK = 64
C_CONV = HC * (2 * DK + DV)
N_BRANCH = 3
BRANCH_W = HA * DA
Q_BLOCK = 128
N_GROUPS = 4
EXPERTS_PER_GROUP = 8
N_EXPERTS = N_GROUPS * EXPERTS_PER_GROUP
TOPK_IN_GROUP = 2
D_EXPERT = 256
RMS_EPS = 1e-6
SPLIT_SIZES = (HA * DA, HA * DA, HA * DA, HB * DB, HB * DB, HB * DB, C_CONV, HC * DV, HC, HC, N_BRANCH * D_MODEL)
D_IN = sum(SPLIT_SIZES)

kernel_name = 'hybrid_diffattn_moba_gdn_hmoe_step'


def rmsnorm(x, gain):
    xf = x.astype(jnp.float32)
    y = xf * lax.rsqrt(jnp.mean(xf * xf, axis=-1, keepdims=True) + RMS_EPS)
    return (y * gain.astype(jnp.float32)).astype(x.dtype)


def l2norm(x):
    xf = x.astype(jnp.float32)
    return xf * lax.rsqrt(jnp.sum(xf * xf, axis=-1, keepdims=True) + 1e-6)


def alibi_slopes():
    n = HA + HB
    s = jnp.asarray(2.0 ** (-8.0 * np.arange(1, n + 1) / n), dtype=jnp.float32)
    return s[0::2], s[1::2]


def over_query_blocks(fn, q, pos, block):
    b, nq = q.shape[0], q.shape[1]
    if nq <= block or nq % block != 0:
        return fn(q, pos)
    n = nq // block
    qb = jnp.moveaxis(q.reshape((b, n, block) + q.shape[2:]), 1, 0)
    out = lax.map(lambda a: fn(a[0], a[1]), (qb, pos.reshape(n, block)))
    return jnp.moveaxis(out, 0, 1).reshape((b, nq) + out.shape[3:])


def diff_attention(q, k, v, q_pos, slopes, lam):
    b, nq, h, _ = q.shape
    nk = k.shape[1]
    q2 = q.reshape(b, nq, h, 2, DA_HALF)
    k2 = k.reshape(b, nk, h, 2, DA_HALF)
    s = jnp.einsum('bqhmd,bkhmd->bhmqk', q2, k2).astype(jnp.float32) * (DA_HALF ** -0.5)
    dist = (q_pos[:, None] - jnp.arange(nk, dtype=jnp.int32)[None, :]).astype(jnp.float32)
    s = jnp.where(dist >= 0, s - slopes[None, :, None, None, None] * dist, -jnp.inf)
    p = jax.nn.softmax(s, axis=-1)
    p = p[:, :, 0] - lam * p[:, :, 1]
    return jnp.einsum('bhqk,bkhd->bqhd', p.astype(v.dtype), v)


def moba_attention(q, k, v, q_pos, slopes):
    b, nq, h, d = q.shape
    nk = k.shape[1]
    nb = -(-nk // MOBA_BLOCK)
    pad = ((0, 0), (0, nb * MOBA_BLOCK - nk), (0, 0), (0, 0))
    kb = jnp.pad(k, pad).reshape(b, nb, MOBA_BLOCK, h, d).transpose(0, 3, 1, 2, 4)
    vb = jnp.pad(v, pad).reshape(b, nb, MOBA_BLOCK, h, d).transpose(0, 3, 1, 2, 4)
    means = jnp.mean(kb.astype(jnp.float32), axis=3)
    n_sel = min(MOBA_TOPK, nb)
    n_slot = n_sel + 1
    b_idx = jnp.arange(b)[:, None, None, None]
    h_idx = jnp.arange(h)[None, :, None, None]
    offs = jnp.arange(MOBA_BLOCK, dtype=jnp.int32)

    def attend(qc, pc):
        nqc = qc.shape[1]
        own = pc // MOBA_BLOCK
        gate = jnp.einsum('bqhd,bhnd->bhqn', qc.astype(jnp.float32), means)
        eligible = jnp.arange(nb, dtype=jnp.int32)[None, :] < own[:, None]
        gate = jnp.where(eligible, gate, -jnp.inf)
        _, top = lax.top_k(gate, n_sel)
        blocks = jnp.concatenate([top.astype(jnp.int32), jnp.broadcast_to(own[None, None, :, None], (b, h, nqc, 1))], axis=-1)
        slot_ok = jnp.concatenate([jnp.arange(n_sel)[None, :] < jnp.minimum(own, MOBA_TOPK)[:, None], jnp.ones((nqc, 1), dtype=bool)], axis=-1)
        kg = kb[b_idx, h_idx, blocks].reshape(b, h, nqc, n_slot * MOBA_BLOCK, d)
        vg = vb[b_idx, h_idx, blocks].reshape(b, h, nqc, n_slot * MOBA_BLOCK, d)
        kpos = (blocks[..., None] * MOBA_BLOCK + offs).reshape(b, h, nqc, n_slot * MOBA_BLOCK)
        dist = (pc[:, None] - kpos).astype(jnp.float32)
        ok = jnp.repeat(slot_ok, MOBA_BLOCK, axis=-1) & (dist >= 0)
        s = jnp.einsum('bqhd,bhqkd->bhqk', qc, kg).astype(jnp.float32) * (d ** -0.5)
        s = jnp.where(ok, s - slopes[None, :, None, None] * dist, -jnp.inf)
        p = jax.nn.softmax(s, axis=-1)
        return jnp.einsum('bhqk,bhqkd->bqhd', p.astype(vg.dtype), vg)

    return over_query_blocks(attend, q, q_pos, MOBA_Q_CHUNK)


def gated_delta_rule(q, k, v, beta, g, s0, chunk):
    b, t, h, _ = q.shape
    n = t // chunk

    def chunks(a):
        a = a.astype(jnp.float32).reshape((b, n, chunk) + a.shape[2:])
        return jnp.moveaxis(jnp.moveaxis(a, 3, 2), 1, 0)

    q, k, v, beta, g = chunks(q), chunks(k), chunks(v), chunks(beta), chunks(g)
    gc = jnp.cumsum(g, axis=-1)
    diff = gc[..., :, None] - gc[..., None, :]
    idx = jnp.arange(chunk)
    dec_strict = jnp.exp(jnp.where(idx[:, None] > idx[None, :], diff, -jnp.inf))
    dec_incl = jnp.exp(jnp.where(idx[:, None] >= idx[None, :], diff, -jnp.inf))
    kbeta = k * beta[..., None]
    eye = jnp.eye(chunk, dtype=jnp.float32)
    a_mat = jnp.einsum('...id,...jd->...ij', kbeta, k) * dec_strict + eye
    t_inv = lax.linalg.triangular_solve(a_mat, jnp.broadcast_to(eye, a_mat.shape), left_side=True, lower=True)
    u = jnp.einsum('...ij,...je->...ie', t_inv, v * beta[..., None])
    w = jnp.einsum('...ij,...jd->...id', t_inv, kbeta * jnp.exp(gc)[..., None])
    qk = jnp.einsum('...id,...jd->...ij', q, k) * dec_incl
    q_dec = q * jnp.exp(gc)[..., None]
    k_dec = k * jnp.exp(gc[..., -1:] - gc)[..., None]
    g_last = jnp.exp(gc[..., -1])

    def step(s, xs):
        u_c, w_c, qk_c, qd_c, kd_c, gl_c = xs
        v_new = u_c - jnp.einsum('bhcd,bhde->bhce', w_c, s)
        o = jnp.einsum('bhcd,bhde->bhce', qd_c, s) + jnp.einsum('bhij,bhje->bhie', qk_c, v_new)
        s = s * gl_c[..., None, None] + jnp.einsum('bhcd,bhce->bhde', kd_c, v_new)
        return s, o

    s_final, o = lax.scan(step, s0.astype(jnp.float32), (u, w, qk, q_dec, k_dec, g_last))
    o = jnp.moveaxis(jnp.moveaxis(o, 0, 1), 2, 3).reshape(b, t, h, v.shape[-1])
    return o, s_final


def hier_moe(h, w_rg, w_re, w_g, w_u, w_d):
    b, t, _ = h.shape
    pg = jax.nn.softmax(jnp.einsum('btd,dg->btg', h, w_rg).astype(jnp.float32), axis=-1)
    pg_top, g_top = lax.top_k(pg, 1)
    le = jnp.einsum('btd,de->bte', h, w_re).astype(jnp.float32).reshape(b, t, N_GROUPS, EXPERTS_PER_GROUP)
    le_g = jnp.take_along_axis(le, g_top[..., None], axis=2)[:, :, 0]
    pe_top, e_top = lax.top_k(jax.nn.softmax(le_g, axis=-1), TOPK_IN_GROUP)
    w_sel = pg_top * pe_top / jnp.sum(pe_top, axis=-1, keepdims=True)
    eid = g_top * EXPERTS_PER_GROUP + e_top
    comb = jnp.sum(jax.nn.one_hot(eid, N_EXPERTS, dtype=jnp.float32) * w_sel[..., None], axis=2)
    a = jax.nn.silu(jnp.einsum('btd,edf->btef', h, w_g)) * jnp.einsum('btd,edf->btef', h, w_u)
    a = a * comb[..., None].astype(a.dtype)
    return jnp.einsum('btef,efd->btd', a, w_d)


def trunk_layer(x, pos, past_a, past_b, conv_prev, s_prev, lw, lam_init):
    b, t, _ = x.shape
    slopes_a, slopes_b = alibi_slopes()
    h = rmsnorm(x, lw['norm_mixer'])
    offs = [int(o) for o in np.cumsum(SPLIT_SIZES)[:-1]]
    a_q, a_k, a_v, b_q, b_k, b_v, c_qkv, c_g, c_beta, c_alpha, gate_logits = jnp.split(h @ lw['w_in'], offs, axis=-1)

    kv_a = jnp.stack([a_k.reshape(b, t, HA, DA), a_v.reshape(b, t, HA, DA)], axis=2)
    kv_a_all = jnp.concatenate([past_a.astype(kv_a.dtype), kv_a], axis=1)
    k_a, v_a = kv_a_all[:, :, 0], kv_a_all[:, :, 1]
    dl = lw['diff_lambda'].astype(jnp.float32)
    lam = jnp.exp(jnp.sum(dl[0] * dl[1])) - jnp.exp(jnp.sum(dl[2] * dl[3])) + lam_init
    out_a = over_query_blocks(lambda qb, pb: diff_attention(qb, k_a, v_a, pb, slopes_a, lam), a_q.reshape(b, t, HA, DA), pos, Q_BLOCK)
    out_a = rmsnorm(out_a, lw['diff_norm_gain']) * (1.0 - lam_init)

    kv_b = jnp.stack([b_k.reshape(b, t, HB, DB), b_v.reshape(b, t, HB, DB)], axis=2)
    kv_b_all = jnp.concatenate([past_b.astype(kv_b.dtype), kv_b], axis=1)
    out_b = moba_attention(b_q.reshape(b, t, HB, DB), kv_b_all[:, :, 0], kv_b_all[:, :, 1], pos, slopes_b)

    c_in = jnp.concatenate([conv_prev.astype(c_qkv.dtype), c_qkv], axis=1)
    conv_new = c_in[:, t:]
    w_conv = lw['gdn_conv_w']
    c = w_conv[0] * c_in[:, 0:t]
    for j in range(1, CONV_W):
        c = c + w_conv[j] * c_in[:, j:j + t]
    c = jax.nn.silu(c)
    c_q, c_k, c_v = jnp.split(c, [HC * DK, 2 * HC * DK], axis=-1)
    q_c = l2norm(c_q.reshape(b, t, HC, DK)) * (DK ** -0.5)
    k_c = l2norm(c_k.reshape(b, t, HC, DK))
    beta = jax.nn.sigmoid(c_beta.astype(jnp.float32))
    g = -jnp.exp(lw['gdn_a_log'].astype(jnp.float32)) * jax.nn.softplus(c_alpha.astype(jnp.float32) + lw['gdn_dt_bias'].astype(jnp.float32))
    chunk = GDN_CHUNK if t % GDN_CHUNK == 0 else t
    o_c, s_new = gated_delta_rule(q_c, k_c, c_v.reshape(b, t, HC, DV), beta, g, s_prev, chunk)
    out_c = rmsnorm(o_c, lw['gdn_norm_gain']).astype(x.dtype) * jax.nn.silu(c_g.reshape(b, t, HC, DV))

    branches = jnp.stack([out_a.reshape(b, t, BRANCH_W), out_b.reshape(b, t, BRANCH_W), out_c.reshape(b, t, BRANCH_W)], axis=2)
    proj = jnp.einsum('btnc,ncd->btnd', branches, lw['w_branch'])
    gates = jax.nn.sigmoid(gate_logits.reshape(b, t, N_BRANCH, D_MODEL))
    x = x + jnp.sum(gates * proj, axis=2) @ lw['w_out']
    x = x + hier_moe(rmsnorm(x, lw['norm_ffn']), lw['router_group'], lw['router_expert'], lw['expert_w_gate'], lw['expert_w_up'], lw['expert_w_down'])
    return x, kv_a, kv_b, s_new.astype(s_prev.dtype), conv_new


def setup_inputs(seed: int = 0) -> dict:
    key = jax.random.key(seed)
    ks = jax.random.split(key, 24)
    f32 = jnp.float32
    n_pages = PAST_LEN // PAGE_SIZE
    n_used = DEC_BATCH * n_pages
    n_pool = n_used + max(1, n_used // 4)

    def nrm(k, shape, scale):
        return jax.random.normal(k, shape, f32) * scale

    def gain(k, shape):
        return 1.0 + 0.01 * jax.random.normal(k, shape, f32)

    page_table = jax.random.permutation(ks[6], n_pool)[:n_used].reshape(DEC_BATCH, n_pages).astype(jnp.int32)
    dt = jnp.exp(jax.random.uniform(ks[13], (DEPTH, HC), f32, math.log(1e-3), math.log(1e-1)))
    return {
        'x_prompt': nrm(ks[0], (BATCH, SEQ, D_MODEL), 1.0),
        'x_sample': nrm(ks[1], (DEC_BATCH, DEC_SEQ, D_MODEL), 1.0),
        'cache_diff_kv': nrm(ks[2], (DEPTH, n_pool, PAGE_SIZE, 2, HA, DA), 1.0),
        'cache_moba_kv': nrm(ks[3], (DEPTH, n_pool, PAGE_SIZE, 2, HB, DB), 1.0),
        'state_gdn': nrm(ks[4], (DEPTH, DEC_BATCH, HC, DK, DV), DK ** -0.5),
        'state_conv': nrm(ks[5], (DEPTH, DEC_BATCH, CONV_W - 1, C_CONV), 1.0),
        'page_table': page_table,
        'norm_mixer': gain(ks[7], (DEPTH, D_MODEL)),
        'w_in': nrm(ks[8], (DEPTH, D_MODEL, D_IN), D_MODEL ** -0.5),
        'diff_lambda': nrm(ks[9], (DEPTH, 4, DA_HALF), 0.1),
        'diff_norm_gain': gain(ks[10], (DEPTH, DA)),
        'gdn_conv_w': nrm(ks[11], (DEPTH, CONV_W, C_CONV), CONV_W ** -0.5),
        'gdn_a_log': jnp.log(jax.random.uniform(ks[12], (DEPTH, HC), f32, 1.0, 16.0)),
        'gdn_dt_bias': dt + jnp.log(-jnp.expm1(-dt)),
        'gdn_norm_gain': gain(ks[14], (DEPTH, DV)),
        'w_branch': nrm(ks[15], (DEPTH, N_BRANCH, BRANCH_W, D_MODEL), BRANCH_W ** -0.5),
        'w_out': nrm(ks[16], (DEPTH, D_MODEL, D_MODEL), D_MODEL ** -0.5),
        'norm_ffn': gain(ks[17], (DEPTH, D_MODEL)),
        'router_group': nrm(ks[18], (DEPTH, D_MODEL, N_GROUPS), D_MODEL ** -0.5),
        'router_expert': nrm(ks[19], (DEPTH, D_MODEL, N_EXPERTS), D_MODEL ** -0.5),
        'expert_w_gate': nrm(ks[20], (DEPTH, N_EXPERTS, D_MODEL, D_EXPERT), D_MODEL ** -0.5),
        'expert_w_up': nrm(ks[21], (DEPTH, N_EXPERTS, D_MODEL, D_EXPERT), D_MODEL ** -0.5),
        'expert_w_down': nrm(ks[22], (DEPTH, N_EXPERTS, D_EXPERT, D_MODEL), D_EXPERT ** -0.5),
        'norm_final': gain(ks[23], (D_MODEL,)),
    }


def reference(x_prompt, x_sample, cache_diff_kv, cache_moba_kv, state_gdn, state_conv, page_table,
              norm_mixer, w_in, diff_lambda, diff_norm_gain, gdn_conv_w, gdn_a_log, gdn_dt_bias, gdn_norm_gain,
              w_branch, w_out, norm_ffn, router_group, router_expert, expert_w_gate, expert_w_up, expert_w_down,
              norm_final):
    bp, tp, _ = x_prompt.shape
    bs, ts, _ = x_sample.shape
    past_len = page_table.shape[1] * cache_diff_kv.shape[2]
    pos_p = jnp.arange(tp, dtype=jnp.int32)
    pos_s = past_len + jnp.arange(ts, dtype=jnp.int32)
    xp, xs = x_prompt, x_sample
    p_kva, p_kvb, p_sg, p_sc = [], [], [], []
    s_kva, s_kvb, s_sg, s_sc = [], [], [], []
    for l in range(DEPTH):
        lw = {
            'norm_mixer': norm_mixer[l], 'w_in': w_in[l], 'diff_lambda': diff_lambda[l],
            'diff_norm_gain': diff_norm_gain[l], 'gdn_conv_w': gdn_conv_w[l], 'gdn_a_log': gdn_a_log[l],
            'gdn_dt_bias': gdn_dt_bias[l], 'gdn_norm_gain': gdn_norm_gain[l], 'w_branch': w_branch[l],
            'w_out': w_out[l], 'norm_ffn': norm_ffn[l], 'router_group': router_group[l],
            'router_expert': router_expert[l], 'expert_w_gate': expert_w_gate[l],
            'expert_w_up': expert_w_up[l], 'expert_w_down': expert_w_down[l],
        }
        lam_init = 0.8 - 0.6 * math.exp(-0.3 * l)
        xp, kva, kvb, sg, sc = trunk_layer(
            xp, pos_p,
            jnp.zeros((bp, 0, 2, HA, DA), xp.dtype), jnp.zeros((bp, 0, 2, HB, DB), xp.dtype),
            jnp.zeros((bp, CONV_W - 1, C_CONV), state_conv.dtype), jnp.zeros((bp, HC, DK, DV), state_gdn.dtype),
            lw, lam_init)
        p_kva.append(kva); p_kvb.append(kvb); p_sg.append(sg); p_sc.append(sc)
        past_a = cache_diff_kv[l, page_table].reshape(bs, past_len, 2, HA, DA)
        past_b = cache_moba_kv[l, page_table].reshape(bs, past_len, 2, HB, DB)
        xs, kva, kvb, sg, sc = trunk_layer(xs, pos_s, past_a, past_b, state_conv[l], state_gdn[l], lw, lam_init)
        s_kva.append(kva); s_kvb.append(kvb); s_sg.append(sg); s_sc.append(sc)
    y_prompt = rmsnorm(xp, norm_final)
    y_sample = rmsnorm(xs, norm_final)
    return (y_prompt, y_sample,
            jnp.stack(p_kva), jnp.stack(p_kvb), jnp.stack(p_sg), jnp.stack(p_sc),
            jnp.stack(s_kva), jnp.stack(s_kvb), jnp.stack(s_sg), jnp.stack(s_sc))
```

```python
import functools
import math

import jax
import jax.numpy as jnp
from jax import lax
from jax.experimental import pallas as pl
from jax.experimental.pallas import tpu as pltpu

HA, DA = 4, 128
DA_HALF = DA // 2
HB, DB = 4, 128
MOBA_BLOCK = 256
MOBA_TOPK = 3
HC, DK, DV = 4, 128, 128
CONV_W = 4
GDN_CHUNK = 64
C_CONV = HC * (2 * DK + DV)
N_BRANCH = 3
BRANCH_W = HA * DA
N_GROUPS = 4
EXPERTS_PER_GROUP = 8
N_EXPERTS = N_GROUPS * EXPERTS_PER_GROUP
D_EXPERT = 256
RMS_EPS = 1e-6

LANES = 128
SUBLANES = 8
VMEM_LIMIT_BYTES = 56 * 1024 * 1024

NEG = -1e30
F32 = jnp.float32
BF16 = jnp.bfloat16
HIGHEST = lax.Precision.HIGHEST

_SPLITS = (HA * DA, HA * DA, HA * DA, HB * DB, HB * DB, HB * DB, C_CONV, HC * DV, HC, HC)
_OFF = [0]
for _s in _SPLITS:
    _OFF.append(_OFF[-1] + _s)
N_MIX_COLS = _OFF[-1]
N_MIX_PAD = -(-N_MIX_COLS // LANES) * LANES
BA_COL = _OFF[8]

ROW_TILE = 512
ATTN_TQ = 256
MOE_TILE = 256
PAGES_PER_STEP = 8
MEAN_PAGES_PER_STEP = 16


def _alibi_slopes():
    n = HA + HB
    s = [2.0 ** (-8.0 * i / n) for i in range(1, n + 1)]
    return s[0::2], s[1::2]


def _cparams(semantics):
    return pltpu.CompilerParams(dimension_semantics=semantics, vmem_limit_bytes=VMEM_LIMIT_BYTES)


def _row_tile(n, pref):
    return pref if n % pref == 0 else n


def _resident(shape):
    zeros = (0,) * len(shape)
    return pl.BlockSpec(shape, lambda *_: zeros, pipeline_mode=pl.Buffered(1))


def _rms(x, gain_row):
    return x * lax.rsqrt(jnp.mean(x * x, axis=-1, keepdims=True) + RMS_EPS) * gain_row


def _dot(a, b, **kw):
    return jnp.dot(a, b, preferred_element_type=F32, **kw)


def _dot_nt(a, b, **kw):
    return lax.dot_general(a, b, (((1,), (1,)), ((), ())), preferred_element_type=F32, **kw)


def _dot_tn(a, b, **kw):
    return lax.dot_general(a, b, (((0,), (0,)), ((), ())), preferred_element_type=F32, **kw)


def _inproj_kernel(x_ref, g_ref, w_ref, qa_ref, kva_ref, qb_ref, kvb_ref, cqkv_ref, cg_ref, ba_ref):
    h = _rms(x_ref[...], g_ref[...]).astype(BF16)

    def mm(lo, hi):
        return _dot(h, w_ref[:, lo:hi])

    qa_ref[...] = mm(_OFF[0], _OFF[1]).astype(qa_ref.dtype)
    kva_ref[...] = mm(_OFF[1], _OFF[3])
    qb_ref[...] = mm(_OFF[3], _OFF[4]).astype(qb_ref.dtype)
    kvb_ref[...] = mm(_OFF[4], _OFF[6])
    cqkv_ref[...] = mm(_OFF[6], _OFF[7])
    cg_ref[...] = mm(_OFF[7], _OFF[8])
    ba_ref[...] = mm(BA_COL, N_MIX_PAD)


def _inproj(x2d, gain_row, w_mix, q_dtype):
    n, d = x2d.shape
    tm = _row_tile(n, ROW_TILE)
    widths = (HA * DA, 2 * HA * DA, HB * DB, 2 * HB * DB, C_CONV, HC * DV, N_MIX_PAD - BA_COL)
    dtypes = (q_dtype, F32, q_dtype, F32, F32, F32, F32)
    return pl.pallas_call(
        _inproj_kernel,
        grid=(n // tm,),
        in_specs=[pl.BlockSpec((tm, d), lambda i: (i, 0)), _resident((1, d)), _resident(w_mix.shape)],
        out_specs=[pl.BlockSpec((tm, w), lambda i: (i, 0)) for w in widths],
        out_shape=[jax.ShapeDtypeStruct((n, w), dt) for w, dt in zip(widths, dtypes)],
        compiler_params=_cparams(("parallel",)),
        name="inproj",
    )(x2d, gain_row, w_mix)


def _diff_lambda(dl, lam_init):
    s1 = jnp.sum(dl[0:1, :] * dl[1:2, :], axis=1, keepdims=True)
    s2 = jnp.sum(dl[2:3, :] * dl[3:4, :], axis=1, keepdims=True)
    return jnp.exp(s1) - jnp.exp(s2) + lam_init


def _split_halves(q):
    lane = lax.broadcasted_iota(jnp.int32, q.shape, 1)
    zero = jnp.zeros_like(q)
    return jnp.concatenate([jnp.where(lane < DA_HALF, q, zero), jnp.where(lane >= DA_HALF, q, zero)], axis=0)


def _online_update(m, l, acc, s, v_bf16):
    m_new = jnp.maximum(m, jnp.max(s, axis=-1, keepdims=True))
    alpha = jnp.exp(m - m_new)
    p = jnp.exp(s - m_new)
    l_new = alpha * l + jnp.sum(p, axis=-1, keepdims=True)
    acc_new = alpha * acc + _dot(p.astype(BF16), v_bf16)
    return m_new, l_new, acc_new


def _bf16_round(x):
    return x.astype(BF16).astype(F32)


def _fresh_keys_init(q, kn, vn, tok, slope, scale, ts):
    cols = []
    for j in range(ts):
        sj = jnp.sum(q * kn[j:j + 1, :], axis=-1, keepdims=True) * scale
        cols.append(jnp.where(tok >= j, sj - slope * (tok - j).astype(F32), NEG))
    m = functools.reduce(jnp.maximum, cols)
    ps = [jnp.exp(c - m) for c in cols]
    l = functools.reduce(jnp.add, ps)
    acc = functools.reduce(jnp.add, [_bf16_round(p) * vn[j:j + 1, :] for j, p in enumerate(ps)])
    return m, l, acc


def _top_blocks(gate, eligible):
    lane = lax.broadcasted_iota(jnp.int32, gate.shape, 1).astype(F32)
    g = jnp.where(eligible, gate, NEG)
    sel = jnp.zeros(gate.shape, F32)
    for _ in range(MOBA_TOPK):
        mx = jnp.max(g, axis=-1, keepdims=True)
        first = jnp.min(jnp.where(g == mx, lane, float(LANES)), axis=-1, keepdims=True)
        hit = (lane == first) & (mx > 0.5 * NEG)
        sel = jnp.where(hit, 1.0, sel)
        g = jnp.where(hit, NEG, g)
    return sel


def _lane_column(mat, n):
    lane = lax.broadcasted_iota(jnp.int32, mat.shape, 1)
    return jnp.sum(jnp.where(lane == n, mat, 0.0), axis=-1, keepdims=True)


def _diff_prompt_kernel(q_ref, k_ref, v_ref, dl_ref, gain_ref, slope_ref, o_ref, kbf, vbf, *, tq, lam_init):
    qi = pl.program_id(2)

    @pl.when(qi == 0)
    def _():
        kbf[...] = k_ref[...].astype(BF16)
        vbf[...] = v_ref[...].astype(BF16)

    q2 = _split_halves(q_ref[...])
    slope = slope_ref[:, 0:1]
    row = lax.broadcasted_iota(jnp.int32, (2 * tq, tq), 0)
    col = lax.broadcasted_iota(jnp.int32, (2 * tq, tq), 1)
    rel = jnp.where(row < tq, row, row - tq) - col
    scale = DA_HALF ** -0.5

    def body(j, carry):
        m, l, acc = carry
        k0 = pl.multiple_of(j * tq, tq)
        s = _dot_nt(q2, kbf[pl.ds(k0, tq), :]) * scale
        dist = rel + (qi - j) * tq
        s = jnp.where(dist >= 0, s - slope * dist.astype(F32), NEG)
        return _online_update(m, l, acc, s, vbf[pl.ds(k0, tq), :])

    m0 = jnp.full((2 * tq, 1), NEG, F32)
    l0 = jnp.zeros((2 * tq, 1), F32)
    a0 = jnp.zeros((2 * tq, DA), F32)
    _, l, acc = lax.fori_loop(0, qi + 1, body, (m0, l0, a0))
    o = acc / l
    lam = _diff_lambda(dl_ref[...], lam_init)
    out = o[:tq] - lam * o[tq:]
    o_ref[...] = (_rms(out, gain_ref[...]) * (1.0 - lam_init)).astype(o_ref.dtype)


def _diff_attn_prompt(q, kv, dl, gain_row, slopes, lam_init, b, t):
    n = b * t
    tq = ATTN_TQ
    assert t % tq == 0
    nq = t // tq
    kern = functools.partial(_diff_prompt_kernel, tq=tq, lam_init=lam_init)
    return pl.pallas_call(
        kern,
        grid=(b, HA, nq),
        in_specs=[
            pl.BlockSpec((tq, DA), lambda bi, h, qi: (bi * nq + qi, h)),
            pl.BlockSpec((t, DA), lambda bi, h, qi: (bi, h)),
            pl.BlockSpec((t, DA), lambda bi, h, qi: (bi, HA + h)),
            pl.BlockSpec((4, DA_HALF), lambda bi, h, qi: (0, 0)),
            pl.BlockSpec((1, DA), lambda bi, h, qi: (0, 0)),
            pl.BlockSpec((None, 1, LANES), lambda bi, h, qi: (h, 0, 0)),
        ],
        out_specs=pl.BlockSpec((tq, DA), lambda bi, h, qi: (bi * nq + qi, h)),
        out_shape=jax.ShapeDtypeStruct((n, HA * DA), BF16),
        scratch_shapes=[pltpu.VMEM((t, DA), BF16), pltpu.VMEM((t, DA), BF16)],
        compiler_params=_cparams(("parallel", "parallel", "arbitrary")),
        name="diff_attn_prompt",
    )(q, kv, kv, dl, gain_row, slopes)


def _moba_prompt_kernel(q_ref, k_ref, v_ref, slope_ref, o_ref, kbf, vbf, means, *, tq, nb):
    qi = pl.program_id(2)

    @pl.when(qi == 0)
    def _():
        kbf[...] = k_ref[...].astype(BF16)
        vbf[...] = v_ref[...].astype(BF16)
        means[...] = jnp.zeros(means.shape, F32)
        for n in range(nb):
            blk = k_ref[n * MOBA_BLOCK:(n + 1) * MOBA_BLOCK, :]
            means[n:n + 1, :] = jnp.sum(blk, axis=0, keepdims=True) * (1.0 / MOBA_BLOCK)

    q = q_ref[...]
    slope = slope_ref[:, 0:1]
    scale = DB ** -0.5
    gate = _dot_nt(q, means[...].astype(BF16))
    lane = lax.broadcasted_iota(jnp.int32, gate.shape, 1)
    sel = _top_blocks(gate, lane < qi)
    row = lax.broadcasted_iota(jnp.int32, (tq, tq), 0)
    col = lax.broadcasted_iota(jnp.int32, (tq, tq), 1)
    rel = row - col

    k0 = pl.multiple_of(qi * tq, tq)
    s = _dot_nt(q, kbf[pl.ds(k0, tq), :]) * scale
    s = jnp.where(rel >= 0, s - slope * rel.astype(F32), NEG)
    carry = _online_update(jnp.full((tq, 1), NEG, F32), jnp.zeros((tq, 1), F32), jnp.zeros((tq, DB), F32),
                           s, vbf[pl.ds(k0, tq), :])

    def body(n, carry):
        m, l, acc = carry
        kn = pl.multiple_of(n * tq, tq)
        s = _dot_nt(q, kbf[pl.ds(kn, tq), :]) * scale
        dist = rel + (qi - n) * tq
        picked = _lane_column(sel, n) > 0.5
        s = jnp.where(picked, s - slope * dist.astype(F32), NEG)
        return _online_update(m, l, acc, s, vbf[pl.ds(kn, tq), :])

    _, l, acc = lax.fori_loop(0, qi, body, carry)
    o_ref[...] = (acc / l).astype(o_ref.dtype)


def _moba_attn_prompt(q, kv, slopes, b, t):
    n = b * t
    tq = ATTN_TQ
    assert tq == MOBA_BLOCK and t % tq == 0
    nq = t // tq
    assert nq <= LANES
    kern = functools.partial(_moba_prompt_kernel, tq=tq, nb=nq)
    return pl.pallas_call(
        kern,
        grid=(b, HB, nq),
        in_specs=[
            pl.BlockSpec((tq, DB), lambda bi, h, qi: (bi * nq + qi, h)),
            pl.BlockSpec((t, DB), lambda bi, h, qi: (bi, h)),
            pl.BlockSpec((t, DB), lambda bi, h, qi: (bi, HB + h)),
            pl.BlockSpec((None, 1, LANES), lambda bi, h, qi: (h, 0, 0)),
        ],
        out_specs=pl.BlockSpec((tq, DB), lambda bi, h, qi: (bi * nq + qi, h)),
        out_shape=jax.ShapeDtypeStruct((n, HB * DB), BF16),
        scratch_shapes=[pltpu.VMEM((t, DB), BF16), pltpu.VMEM((t, DB), BF16), pltpu.VMEM((LANES, DB), F32)],
        compiler_params=_cparams(("parallel", "parallel", "arbitrary")),
        name="moba_attn_prompt",
    )(q, kv, kv, slopes)


def _page_rows(pg_ref, kv, h, n_heads, page):
    return pg_ref[pl.ds(kv * n_heads + h, page, stride=2 * n_heads), :]


def _diff_sample_kernel(pt_ref, q_ref, kvn_ref, dl_ref, gain_ref, *rest, n_in, ts, page, past_len, lam_init,
                        slopes):
    del pt_ref
    pages = rest[:n_in]
    o_ref = rest[n_in]
    m_s, l_s, acc_s = rest[n_in + 1:]
    i = pl.program_id(1)
    rows = 2 * ts
    row1 = lax.broadcasted_iota(jnp.int32, (rows, 1), 0)
    tok1 = jnp.where(row1 < ts, row1, row1 - ts)
    lane = lax.broadcasted_iota(jnp.int32, (rows, page), 1)
    scale = DA_HALF ** -0.5
    q_all = q_ref[...]

    for h in range(HA):
        slope = slopes[h]
        q2 = _split_halves(q_all[:, h * DA:(h + 1) * DA] * scale)

        q2b = q2.astype(BF16)

        @pl.when(i == 0)
        def _(h=h, q2b=q2b, slope=slope):
            kn = _bf16_round(kvn_ref[:, h * DA:(h + 1) * DA])
            vn = _bf16_round(kvn_ref[:, (HA + h) * DA:(HA + h + 1) * DA])
            m, l, acc = _fresh_keys_init(q2b.astype(F32), kn, vn, tok1, slope, 1.0, ts)
            m_s[h], l_s[h], acc_s[h] = m, l, acc

        m, l, acc = m_s[h], l_s[h], acc_s[h]
        for j in range(n_in):
            kh = _page_rows(pages[j], 0, h, HA, page).astype(BF16)
            vh = _page_rows(pages[j], 1, h, HA, page).astype(BF16)
            kpos = (i * n_in + j) * page + lane
            dist = (past_len + tok1) - kpos
            s = _dot_nt(q2b, kh) - slope * dist.astype(F32)
            m, l, acc = _online_update(m, l, acc, s, vh)
        m_s[h], l_s[h], acc_s[h] = m, l, acc

        @pl.when(i == pl.num_programs(1) - 1)
        def _(h=h, l=l, acc=acc):
            o = acc / l
            out = o[:ts] - _diff_lambda(dl_ref[...], lam_init) * o[ts:]
            o_ref[:, h * DA:(h + 1) * DA] = _rms(out, gain_ref[...]) * (1.0 - lam_init)


def _page_specs(layer, n_pages, n_in, rows):
    def spec(j):
        return pl.BlockSpec((None, None, rows, LANES),
                            lambda b, i, pt: (layer, pt[b * n_pages + i * n_in + j], 0, 0))
    return [spec(j) for j in range(n_in)]


def _diff_attn_sample(q3, kvn3, cache4, pt_flat, dl, gain_row, slopes, lam_init, layer, n_pages, page):
    bs, ts, _ = q3.shape
    assert 2 * ts == SUBLANES
    n_in = min(PAGES_PER_STEP, n_pages)
    assert n_pages % n_in == 0
    rows = cache4.shape[2]
    kern = functools.partial(_diff_sample_kernel, n_in=n_in, ts=ts, page=page, past_len=n_pages * page,
                             lam_init=lam_init, slopes=slopes)
    grid_spec = pltpu.PrefetchScalarGridSpec(
        num_scalar_prefetch=1,
        grid=(bs, n_pages // n_in),
        in_specs=[
            pl.BlockSpec((None, ts, HA * DA), lambda b, i, pt: (b, 0, 0)),
            pl.BlockSpec((None, ts, 2 * HA * DA), lambda b, i, pt: (b, 0, 0)),
            pl.BlockSpec((4, DA_HALF), lambda b, i, pt: (0, 0)),
            pl.BlockSpec((1, DA), lambda b, i, pt: (0, 0)),
        ] + _page_specs(layer, n_pages, n_in, rows),
        out_specs=pl.BlockSpec((None, ts, HA * DA), lambda b, i, pt: (b, 0, 0)),
        scratch_shapes=[pltpu.VMEM((HA, 2 * ts, 1), F32), pltpu.VMEM((HA, 2 * ts, 1), F32),
                        pltpu.VMEM((HA, 2 * ts, DA), F32)],
    )
    return pl.pallas_call(
        kern,
        grid_spec=grid_spec,
        out_shape=jax.ShapeDtypeStruct((bs, ts, HA * DA), F32),
        compiler_params=_cparams(("parallel", "arbitrary")),
        name="diff_attn_sample",
    )(pt_flat, q3, kvn3, dl, gain_row, *([cache4] * n_in))


def _moba_means_kernel(pt_ref, *rest, n_in, pages_per_block, page):
    del pt_ref
    pages = rest[:n_in]
    o_ref = rest[n_in]
    inv = 1.0 / (pages_per_block * page)
    for jb in range(n_in // pages_per_block):
        tot = None
        for j in range(pages_per_block):
            part = jnp.sum(pages[jb * pages_per_block + j][...], axis=0)
            tot = part if tot is None else tot + part
        tot = tot * inv
        for h in range(HB):
            o_ref[h, jb:jb + 1, :] = tot[h:h + 1, :]


def _moba_means_sample(cache6, pt_flat, layer, bs, n_pages, page):
    assert MOBA_BLOCK % page == 0
    ppb = MOBA_BLOCK // page
    n_in = min(MEAN_PAGES_PER_STEP, n_pages)
    assert n_pages % n_in == 0 and n_in % ppb == 0
    nb_step = n_in // ppb
    nb = n_pages // ppb
    assert nb_step % SUBLANES == 0 or nb_step == nb

    def spec(j):
        return pl.BlockSpec((None, None, page, None, HB, DB),
                            lambda b, i, pt: (layer, pt[b * n_pages + i * n_in + j], 0, 0, 0, 0))

    kern = functools.partial(_moba_means_kernel, n_in=n_in, pages_per_block=ppb, page=page)
    grid_spec = pltpu.PrefetchScalarGridSpec(
        num_scalar_prefetch=1,
        grid=(bs, n_pages // n_in),
        in_specs=[spec(j) for j in range(n_in)],
        out_specs=pl.BlockSpec((None, HB, nb_step, DB), lambda b, i, pt: (b, 0, i, 0)),
    )
    return pl.pallas_call(
        kern,
        grid_spec=grid_spec,
        out_shape=jax.ShapeDtypeStruct((bs, HB, nb, DB), F32),
        compiler_params=_cparams(("parallel", "arbitrary")),
        name="moba_means_sample",
    )(pt_flat, *([cache6] * n_in))


def _moba_sample_kernel(pt_ref, q_ref, kvn_ref, means_ref, *rest, n_in, ts, page, past_len, slopes):
    del pt_ref
    pages = rest[:n_in]
    o_ref = rest[n_in]
    sel_s, m_s, l_s, acc_s = rest[n_in + 1:]
    i = pl.program_id(1)
    rows = SUBLANES
    row1 = lax.broadcasted_iota(jnp.int32, (rows, 1), 0)
    tok1 = jnp.where(row1 < ts, row1, row1 - ts)
    lane = lax.broadcasted_iota(jnp.int32, (rows, page), 1)
    scale = DB ** -0.5
    pages_per_block = MOBA_BLOCK // page
    nb_past = past_len // MOBA_BLOCK
    q_all = q_ref[...]

    for h in range(HB):
        slope = slopes[h]
        qh = q_all[:, h * DB:(h + 1) * DB]
        q8 = jnp.concatenate([qh, qh], axis=0)

        q8b = q8.astype(BF16)

        @pl.when(i == 0)
        def _(h=h, q8b=q8b, slope=slope):
            gate = _dot_nt(q8b, means_ref[h].astype(BF16))
            glane = lax.broadcasted_iota(jnp.int32, gate.shape, 1)
            sel_s[h] = _top_blocks(gate, glane < nb_past)
            kn = _bf16_round(kvn_ref[:, h * DB:(h + 1) * DB])
            vn = _bf16_round(kvn_ref[:, (HB + h) * DB:(HB + h + 1) * DB])
            m, l, acc = _fresh_keys_init(q8b.astype(F32), kn, vn, tok1, slope, scale, ts)
            m_s[h], l_s[h], acc_s[h] = m, l, acc

        m, l, acc = m_s[h], l_s[h], acc_s[h]
        sel = sel_s[h]
        for j in range(n_in):
            pg = i * n_in + j
            kh = _page_rows(pages[j], 0, h, HB, page).astype(BF16)
            vh = _page_rows(pages[j], 1, h, HB, page).astype(BF16)
            dist = (past_len + tok1) - (pg * page + lane)
            picked = _lane_column(sel, pg // pages_per_block) > 0.5
            s = jnp.where(picked, _dot_nt(q8b, kh) * scale - slope * dist.astype(F32), NEG)
            m, l, acc = _online_update(m, l, acc, s, vh)
        m_s[h], l_s[h], acc_s[h] = m, l, acc

        @pl.when(i == pl.num_programs(1) - 1)
        def _(h=h, l=l, acc=acc):
            o_ref[:, h * DB:(h + 1) * DB] = (acc / l)[:ts]


def _moba_attn_sample(q3, kvn3, means_pad, cache4, pt_flat, slopes, layer, n_pages, page):
    bs, ts, _ = q3.shape
    assert 2 * ts == SUBLANES
    past_len = n_pages * page
    assert past_len % MOBA_BLOCK == 0 and ts <= MOBA_BLOCK and past_len // MOBA_BLOCK <= LANES
    n_in = min(PAGES_PER_STEP, n_pages)
    assert n_pages % n_in == 0
    rows = cache4.shape[2]
    kern = functools.partial(_moba_sample_kernel, n_in=n_in, ts=ts, page=page, past_len=past_len, slopes=slopes)
    grid_spec = pltpu.PrefetchScalarGridSpec(
        num_scalar_prefetch=1,
        grid=(bs, n_pages // n_in),
        in_specs=[
            pl.BlockSpec((None, ts, HB * DB), lambda b, i, pt: (b, 0, 0)),
            pl.BlockSpec((None, ts, 2 * HB * DB), lambda b, i, pt: (b, 0, 0)),
            pl.BlockSpec((None, HB, LANES, DB), lambda b, i, pt: (b, 0, 0, 0)),
        ] + _page_specs(layer, n_pages, n_in, rows),
        out_specs=pl.BlockSpec((None, ts, HB * DB), lambda b, i, pt: (b, 0, 0)),
        scratch_shapes=[pltpu.VMEM((HB, SUBLANES, LANES), F32), pltpu.VMEM((HB, SUBLANES, 1), F32),
                        pltpu.VMEM((HB, SUBLANES, 1), F32), pltpu.VMEM((HB, SUBLANES, DB), F32)],
    )
    return pl.pallas_call(
        kern,
        grid_spec=grid_spec,
        out_shape=jax.ShapeDtypeStruct((bs, ts, HB * DB), F32),
        compiler_params=_cparams(("parallel", "arbitrary")),
        name="moba_attn_sample",
    )(pt_flat, q3, kvn3, means_pad, *([cache4] * n_in))


def _unit_lower_inverse(nmat):
    c = nmat.shape[0]
    row = lax.broadcasted_iota(jnp.int32, (c, c), 0)
    col = lax.broadcasted_iota(jnp.int32, (c, c), 1)
    eye = (row == col).astype(F32)
    hp = functools.partial(_dot, precision=HIGHEST)
    shift = 3
    same = lax.shift_right_logical(row, shift) == lax.shift_right_logical(col, shift)
    n0 = jnp.where(same, nmat, 0.0)
    n2 = hp(n0, n0)
    n4 = hp(n2, n2)
    inv = hp(hp(eye - n0, eye + n2), eye + n4)
    while (1 << shift) < c:
        shift += 1
        same2 = lax.shift_right_logical(row, shift) == lax.shift_right_logical(col, shift)
        off = jnp.where(same2 & jnp.logical_not(same), nmat, 0.0)
        inv = inv - hp(hp(inv, off), inv)
        same = same2
    return inv


def _gdn_kernel(cqkv_ref, cg_ref, ba_ref, prev_ref, s0_ref, cw_ref, alog_ref, dtb_ref, gain_ref,
                o_ref, s_ref, xbuf, *, chunk, t_valid):
    c = pl.program_id(1)
    halo = SUBLANES

    @pl.when(c == 0)
    def _():
        xbuf[0:halo, :] = prev_ref[...]
        s_ref[...] = s0_ref[...]

    xbuf[halo:halo + chunk, :] = cqkv_ref[...]
    conv = cw_ref[0:1, :] * xbuf[halo - 3:halo - 3 + chunk, :]
    for j in range(1, CONV_W):
        conv = conv + cw_ref[j:j + 1, :] * xbuf[halo - 3 + j:halo - 3 + j + chunk, :]
    xbuf[0:halo, :] = xbuf[chunk:chunk + halo, :]
    conv = conv * jax.nn.sigmoid(conv)

    rowc = lax.broadcasted_iota(jnp.int32, (chunk, 1), 0)
    valid = (c * chunk + rowc) < t_valid
    ba = ba_ref[...]
    beta_all = jnp.where(valid, jax.nn.sigmoid(ba), 0.0)
    g_all = jnp.where(valid, -jnp.exp(alog_ref[...]) * jax.nn.softplus(ba + dtb_ref[...]), 0.0)
    ri = lax.broadcasted_iota(jnp.int32, (chunk, chunk), 0)
    ci = lax.broadcasted_iota(jnp.int32, (chunk, chunk), 1)
    tril = (ri >= ci).astype(F32)
    gc_all = _dot(tril, g_all, precision=HIGHEST)
    lane_b = lax.broadcasted_iota(jnp.int32, (chunk, LANES), 1)

    for h in range(HC):
        qh = conv[:, h * DK:(h + 1) * DK]
        kh = conv[:, (HC + h) * DK:(HC + h + 1) * DK]
        vh = conv[:, 2 * HC * DK + h * DV:2 * HC * DK + (h + 1) * DV]
        qh = qh * lax.rsqrt(jnp.sum(qh * qh, axis=-1, keepdims=True) + 1e-6) * (DK ** -0.5)
        kh = kh * lax.rsqrt(jnp.sum(kh * kh, axis=-1, keepdims=True) + 1e-6)
        beta = _lane_column(beta_all, h)
        gc = _lane_column(gc_all, HC + h)
        pick = (lane_b == HC + h).astype(F32)
        gc_row = _dot_nt(pick, gc_all, precision=HIGHEST)
        diff = gc - gc_row
        dec_strict = jnp.exp(jnp.where(ri > ci, diff, NEG))
        dec_incl = jnp.exp(jnp.where(ri >= ci, diff, NEG))
        egc = jnp.exp(gc)
        gc_last = gc_row[:, chunk - 1:chunk]
        kbeta = kh * beta
        kb16, k16 = kbeta.astype(BF16), kh.astype(BF16)
        t_inv = _unit_lower_inverse(_dot_nt(kb16, k16) * dec_strict)
        t16 = t_inv.astype(BF16)
        u = _dot(t16, (vh * beta).astype(BF16))
        w = _dot(t16, (kbeta * egc).astype(BF16))
        qk = _dot_nt(qh.astype(BF16), k16) * dec_incl
        q_dec = qh * egc
        k_dec = kh * jnp.exp(gc_last - gc)
        s = s_ref[h]
        s16 = s.astype(BF16)
        v_new = u - _dot(w.astype(BF16), s16)
        o = _dot(q_dec.astype(BF16), s16) + _dot(qk.astype(BF16), v_new.astype(BF16))
        s_ref[h] = s * jnp.exp(gc_last[0:1, :]) + _dot_tn(k_dec.astype(BF16), v_new.astype(BF16))
        gate = cg_ref[:, h * DV:(h + 1) * DV]
        o_ref[:, h * DV:(h + 1) * DV] = (_rms(o, gain_ref[...]) * (gate * jax.nn.sigmoid(gate))).astype(o_ref.dtype)


def _gdn(cqkv, cg, ba, prev8, s0, conv_w8, alog_row, dtb_row, gain_row, b, t, t_valid):
    chunk = GDN_CHUNK
    assert t % chunk == 0
    nc = t // chunk
    kern = functools.partial(_gdn_kernel, chunk=chunk, t_valid=t_valid)
    return pl.pallas_call(
        kern,
        grid=(b, nc),
        in_specs=[
            pl.BlockSpec((chunk, C_CONV), lambda bi, c: (bi * nc + c, 0)),
            pl.BlockSpec((chunk, HC * DV), lambda bi, c: (bi * nc + c, 0)),
            pl.BlockSpec((chunk, LANES), lambda bi, c: (bi * nc + c, 0)),
            pl.BlockSpec((None, SUBLANES, C_CONV), lambda bi, c: (bi, 0, 0)),
            pl.BlockSpec((None, HC, DK, DV), lambda bi, c: (bi, 0, 0, 0)),
            pl.BlockSpec((SUBLANES, C_CONV), lambda bi, c: (0, 0)),
            pl.BlockSpec((1, LANES), lambda bi, c: (0, 0)),
            pl.BlockSpec((1, LANES), lambda bi, c: (0, 0)),
            pl.BlockSpec((1, DV), lambda bi, c: (0, 0)),
        ],
        out_specs=[
            pl.BlockSpec((chunk, HC * DV), lambda bi, c: (bi * nc + c, 0)),
            pl.BlockSpec((None, HC, DK, DV), lambda bi, c: (bi, 0, 0, 0)),
        ],
        out_shape=[jax.ShapeDtypeStruct((b * t, HC * DV), BF16), jax.ShapeDtypeStruct((b, HC, DK, DV), F32)],
        scratch_shapes=[pltpu.VMEM((chunk + SUBLANES, C_CONV), F32)],
        compiler_params=_cparams(("parallel", "arbitrary")),
        name="gdn",
    )(cqkv, cg, ba, prev8, s0, conv_w8, alog_row, dtb_row, gain_row)


def _route(logits):
    lane = lax.broadcasted_iota(jnp.int32, logits.shape, 1).astype(F32)

    def argmax_first(x):
        mx = jnp.max(x, axis=-1, keepdims=True)
        return mx, jnp.min(jnp.where(x == mx, lane, float(LANES)), axis=-1, keepdims=True)

    lg = jnp.where(lane < N_GROUPS, logits, NEG)
    mg, g_top = argmax_first(lg)
    pg_top = 1.0 / jnp.sum(jnp.exp(lg - mg), axis=-1, keepdims=True)
    lo = N_GROUPS + g_top * EXPERTS_PER_GROUP
    le = jnp.where((lane >= lo) & (lane < lo + EXPERTS_PER_GROUP), logits, NEG)
    m1, i1 = argmax_first(le)
    m2, i2 = argmax_first(jnp.where(lane == i1, NEG, le))
    se = jnp.sum(jnp.exp(le - m1), axis=-1, keepdims=True)
    pe1 = 1.0 / se
    pe2 = jnp.exp(m2 - m1) / se
    w1 = pg_top * pe1 / (pe1 + pe2)
    w2 = pg_top * pe2 / (pe1 + pe2)
    out = jnp.where(lane == 0, i1 - N_GROUPS, 0.0)
    out = jnp.where(lane == 1, i2 - N_GROUPS, out)
    out = jnp.where(lane == 2, w1, out)
    return jnp.where(lane == 3, w2, out)


def _merge_kernel(x_ref, a_ref, b_ref, c_ref, gm_ref, wg_ref, wb_ref, wo_ref, gf_ref, wr_ref,
                  x1_ref, h2_ref, route_ref):
    x = x_ref[...]
    d = x.shape[-1]
    h = _rms(x, gm_ref[...]).astype(BF16)
    mix = None
    for n, br in enumerate((a_ref, b_ref, c_ref)):
        term = jax.nn.sigmoid(_dot(h, wg_ref[:, n * d:(n + 1) * d])) * _dot(br[...], wb_ref[n])
        mix = term if mix is None else mix + term
    x1 = x + _dot(mix.astype(BF16), wo_ref[...])
    x1_ref[...] = x1
    h2 = _rms(x1, gf_ref[...]).astype(BF16)
    h2_ref[...] = h2
    route_ref[...] = _route(_dot(h2, wr_ref[...]))


def _merge(x2d, out_a, out_b, out_c, gm_row, w_gate, w_branch, w_out, gf_row, w_router):
    n, d = x2d.shape
    tm = _row_tile(n, ROW_TILE)
    row = lambda i: (i, 0)
    return pl.pallas_call(
        _merge_kernel,
        grid=(n // tm,),
        in_specs=[
            pl.BlockSpec((tm, d), row), pl.BlockSpec((tm, BRANCH_W), row), pl.BlockSpec((tm, BRANCH_W), row),
            pl.BlockSpec((tm, BRANCH_W), row), _resident((1, d)), _resident(w_gate.shape),
            _resident(w_branch.shape), _resident(w_out.shape), _resident((1, d)), _resident(w_router.shape),
        ],
        out_specs=[pl.BlockSpec((tm, d), row), pl.BlockSpec((tm, d), row), pl.BlockSpec((tm, LANES), row)],
        out_shape=[jax.ShapeDtypeStruct((n, d), F32), jax.ShapeDtypeStruct((n, d), BF16),
                   jax.ShapeDtypeStruct((n, LANES), F32)],
        compiler_params=_cparams(("parallel",)),
        name="merge",
    )(x2d, out_a, out_b, out_c, gm_row, w_gate, w_branch, w_out, gf_row, w_router)


def _moe_kernel(te_ref, xs_ref, rw_ref, wg_ref, wu_ref, wd_ref, y_ref):
    del te_ref
    x = xs_ref[...]
    g = _dot(x, wg_ref[...])
    a = (g * jax.nn.sigmoid(g)) * _dot(x, wu_ref[...]) * rw_ref[...]
    y_ref[...] = _dot(a.astype(BF16), wd_ref[...])


def _moe_experts(xs, row_w, tile_expert, w_g, w_u, w_d):
    m, d = xs.shape
    f = w_g.shape[-1]
    grid_spec = pltpu.PrefetchScalarGridSpec(
        num_scalar_prefetch=1,
        grid=(m // MOE_TILE,),
        in_specs=[
            pl.BlockSpec((MOE_TILE, d), lambda i, te: (i, 0)),
            pl.BlockSpec((MOE_TILE, 1), lambda i, te: (i, 0)),
            pl.BlockSpec((None, d, f), lambda i, te: (te[i], 0, 0)),
            pl.BlockSpec((None, d, f), lambda i, te: (te[i], 0, 0)),
            pl.BlockSpec((None, f, d), lambda i, te: (te[i], 0, 0)),
        ],
        out_specs=pl.BlockSpec((MOE_TILE, d), lambda i, te: (i, 0)),
    )
    return pl.pallas_call(
        _moe_kernel,
        grid_spec=grid_spec,
        out_shape=jax.ShapeDtypeStruct((m, d), F32),
        compiler_params=_cparams(("arbitrary",)),
        name="moe_experts",
    )(tile_expert, xs, row_w, w_g, w_u, w_d)


def _combine_kernel(x_ref, y1_ref, y2_ref, gain_ref, o_ref, *, final_norm):
    x = x_ref[...] + (y1_ref[...] + y2_ref[...])
    o_ref[...] = _rms(x, gain_ref[...]) if final_norm else x


def _combine(x1, y1, y2, gain_row, final_norm):
    n, d = x1.shape
    tm = _row_tile(n, ROW_TILE)
    row = lambda i: (i, 0)
    return pl.pallas_call(
        functools.partial(_combine_kernel, final_norm=final_norm),
        grid=(n // tm,),
        in_specs=[pl.BlockSpec((tm, d), row), pl.BlockSpec((tm, d), row), pl.BlockSpec((tm, d), row),
                  _resident((1, d))],
        out_specs=pl.BlockSpec((tm, d), row),
        out_shape=jax.ShapeDtypeStruct((n, d), F32),
        compiler_params=_cparams(("parallel",)),
        name="moe_combine",
    )(x1, y1, y2, gain_row)


def _dispatch_tables(route, tile):
    n = route.shape[0]
    eid = route[:, 0:2].astype(jnp.int32).reshape(-1)
    wsel = route[:, 2:4].reshape(-1)
    n_assign = 2 * n
    m_pad = -(-(n_assign + N_EXPERTS * (tile - 1)) // tile) * tile
    order = jnp.argsort(eid, stable=True)
    e_sorted = eid[order]
    counts = jnp.zeros((N_EXPERTS,), jnp.int32).at[eid].add(1)
    padded = (counts + tile - 1) // tile * tile
    ends_p = jnp.cumsum(padded)
    starts_p = ends_p - padded
    starts = jnp.cumsum(counts) - counts
    dest = starts_p[e_sorted] + (jnp.arange(n_assign, dtype=jnp.int32) - starts[e_sorted])
    row_src = jnp.zeros((m_pad,), jnp.int32).at[dest].set(order // 2)
    row_w = jnp.zeros((m_pad,), F32).at[dest].set(wsel[order])
    slot_row = jnp.zeros((n_assign,), jnp.int32).at[order].set(dest).reshape(n, 2)
    tile_start = jnp.arange(m_pad // tile, dtype=jnp.int32) * tile
    tile_expert = jnp.minimum(jnp.searchsorted(ends_p, tile_start, side="right"), N_EXPERTS - 1).astype(jnp.int32)
    return row_src, row_w.reshape(m_pad, 1), slot_row, tile_expert


def _moe(x1, h2, route, w_g, w_u, w_d, gain_row, final_norm):
    row_src, row_w, slot_row, tile_expert = _dispatch_tables(route, MOE_TILE)
    ys = _moe_experts(h2[row_src], row_w, tile_expert, w_g, w_u, w_d)
    return _combine(x1, ys[slot_row[:, 0]], ys[slot_row[:, 1]], gain_row, final_norm)


def _lane_row(vec, offset):
    return jnp.zeros((1, LANES), F32).at[0, offset:offset + vec.shape[0]].set(vec.astype(F32))


def _layer_weights(l, w_in, norm_mixer, diff_lambda, diff_norm_gain, gdn_conv_w, gdn_a_log, gdn_dt_bias,
                   gdn_norm_gain, w_branch, w_out, norm_ffn, router_group, router_expert, expert_w_gate,
                   expert_w_up, expert_w_down):
    d = w_in.shape[1]
    w_mix = jnp.pad(w_in[l, :, :N_MIX_COLS], ((0, 0), (0, N_MIX_PAD - N_MIX_COLS))).astype(BF16)
    w_router = jnp.concatenate([router_group[l], router_expert[l]], axis=1)
    w_router = jnp.pad(w_router, ((0, 0), (0, LANES - w_router.shape[1]))).astype(BF16)
    return dict(
        gm=norm_mixer[l].reshape(1, d), w_mix=w_mix, w_gate=w_in[l, :, N_MIX_COLS:].astype(BF16),
        dl=diff_lambda[l].astype(F32), dgain=diff_norm_gain[l].reshape(1, DA).astype(F32),
        conv_w=jnp.pad(gdn_conv_w[l].astype(F32), ((0, SUBLANES - CONV_W), (0, 0))),
        alog=_lane_row(gdn_a_log[l], HC), dtb=_lane_row(gdn_dt_bias[l], HC),
        ggain=gdn_norm_gain[l].reshape(1, DV).astype(F32),
        w_branch=w_branch[l].astype(BF16), w_out=w_out[l].astype(BF16), gf=norm_ffn[l].reshape(1, d),
        w_router=w_router, w_g=expert_w_gate[l].astype(BF16), w_u=expert_w_up[l].astype(BF16),
        w_d=expert_w_down[l].astype(BF16),
    )


def _slope_rows(slopes):
    return jnp.broadcast_to(jnp.asarray(slopes, F32)[:, None, None], (len(slopes), 1, LANES))


def _prompt_layer(x2d, lw, lam_init, b, t, final_gain):
    slopes_a, slopes_b = _alibi_slopes()
    q_a, kv_a, q_b, kv_b, cqkv, cg, ba = _inproj(x2d, lw["gm"], lw["w_mix"], BF16)
    out_a = _diff_attn_prompt(q_a, kv_a, lw["dl"], lw["dgain"], _slope_rows(slopes_a), lam_init, b, t)
    out_b = _moba_attn_prompt(q_b, kv_b, _slope_rows(slopes_b), b, t)
    prev8 = jnp.zeros((b, SUBLANES, C_CONV), F32)
    s0 = jnp.zeros((b, HC, DK, DV), F32)
    out_c, s_new = _gdn(cqkv, cg, ba, prev8, s0, lw["conv_w"], lw["alog"], lw["dtb"], lw["ggain"], b, t, t)
    x1, h2, route = _merge(x2d, out_a, out_b, out_c, lw["gm"], lw["w_gate"], lw["w_branch"], lw["w_out"],
                           lw["gf"], lw["w_router"])
    gain = lw["gf"] if final_gain is None else final_gain
    x2 = _moe(x1, h2, route, lw["w_g"], lw["w_u"], lw["w_d"], gain, final_gain is not None)
    conv_new = cqkv.reshape(b, t, C_CONV)[:, t - (CONV_W - 1):]
    return x2, kv_a, kv_b, s_new, conv_new


def _sample_layer(x2d, lw, lam_init, bs, ts, layer, cache_a, cache_b, pt_flat, n_pages, page, conv_prev, s_prev,
                  final_gain):
    slopes_a, slopes_b = _alibi_slopes()
    n = bs * ts
    q_a, kv_a, q_b, kv_b, cqkv, cg, ba = _inproj(x2d, lw["gm"], lw["w_mix"], F32)
    rows = page * 2 * HA
    out_a = _diff_attn_sample(q_a.reshape(bs, ts, -1), kv_a.reshape(bs, ts, -1),
                              cache_a.reshape(cache_a.shape[0], cache_a.shape[1], rows, DA), pt_flat,
                              lw["dl"], lw["dgain"], slopes_a, lam_init, layer, n_pages, page)
    means = _moba_means_sample(cache_b, pt_flat, layer, bs, n_pages, page)
    means = jnp.pad(means, ((0, 0), (0, 0), (0, LANES - means.shape[2]), (0, 0)))
    out_b = _moba_attn_sample(q_b.reshape(bs, ts, -1), kv_b.reshape(bs, ts, -1), means,
                              cache_b.reshape(cache_b.shape[0], cache_b.shape[1], rows, DB), pt_flat,
                              slopes_b, layer, n_pages, page)
    tpad = GDN_CHUNK

    def pad_t(a):
        return jnp.pad(a.reshape(bs, ts, -1), ((0, 0), (0, tpad - ts), (0, 0))).reshape(bs * tpad, -1)

    prev8 = jnp.pad(conv_prev.astype(F32), ((0, 0), (SUBLANES - (CONV_W - 1), 0), (0, 0)))
    out_c, s_new = _gdn(pad_t(cqkv), pad_t(cg), pad_t(ba), prev8, s_prev.astype(F32), lw["conv_w"], lw["alog"],
                        lw["dtb"], lw["ggain"], bs, tpad, ts)
    out_c = out_c.reshape(bs, tpad, -1)[:, :ts].reshape(n, -1)
    x1, h2, route = _merge(x2d, out_a.reshape(n, -1).astype(BF16), out_b.reshape(n, -1).astype(BF16), out_c,
                           lw["gm"], lw["w_gate"], lw["w_branch"], lw["w_out"], lw["gf"], lw["w_router"])
    gain = lw["gf"] if final_gain is None else final_gain
    x2 = _moe(x1, h2, route, lw["w_g"], lw["w_u"], lw["w_d"], gain, final_gain is not None)
    c_in = jnp.concatenate([conv_prev.astype(F32), cqkv.reshape(bs, ts, C_CONV)], axis=1)
    return x2, kv_a, kv_b, s_new, c_in[:, ts:]


def kernel(x_prompt, x_sample, cache_diff_kv, cache_moba_kv, state_gdn, state_conv, page_table, norm_mixer, w_in, diff_lambda, diff_norm_gain, gdn_conv_w, gdn_a_log, gdn_dt_bias, gdn_norm_gain, w_branch, w_out, norm_ffn, router_group, router_expert, expert_w_gate, expert_w_up, expert_w_down, norm_final):
    bp, tp, d = x_prompt.shape
    bs, ts, _ = x_sample.shape
    depth = w_in.shape[0]
    n_pages = page_table.shape[1]
    page = cache_diff_kv.shape[2]
    pt_flat = page_table.reshape(-1).astype(jnp.int32)
    final_gain = norm_final.reshape(1, d)

    xp = x_prompt.reshape(bp * tp, d)
    xs = x_sample.reshape(bs * ts, d)
    outs_p = [[], [], [], []]
    outs_s = [[], [], [], []]
    for l in range(depth):
        lw = _layer_weights(l, w_in, norm_mixer, diff_lambda, diff_norm_gain, gdn_conv_w, gdn_a_log, gdn_dt_bias,
                            gdn_norm_gain, w_branch, w_out, norm_ffn, router_group, router_expert, expert_w_gate,
                            expert_w_up, expert_w_down)
        lam_init = 0.8 - 0.6 * math.exp(-0.3 * l)
        fg = final_gain if l == depth - 1 else None
        xp, kva, kvb, sg, sc = _prompt_layer(xp, lw, lam_init, bp, tp, fg)
        for acc, v in zip(outs_p, (kva.reshape(bp, tp, 2, HA, DA), kvb.reshape(bp, tp, 2, HB, DB), sg, sc)):
            acc.append(v)
        xs, kva, kvb, sg, sc = _sample_layer(xs, lw, lam_init, bs, ts, l, cache_diff_kv, cache_moba_kv, pt_flat,
                                             n_pages, page, state_conv[l], state_gdn[l], fg)
        for acc, v in zip(outs_s, (kva.reshape(bs, ts, 2, HA, DA), kvb.reshape(bs, ts, 2, HB, DB),
                                   sg.astype(state_gdn.dtype), sc)):
            acc.append(v)
    return (xp.reshape(bp, tp, d), xs.reshape(bs, ts, d),
            jnp.stack(outs_p[0]), jnp.stack(outs_p[1]), jnp.stack(outs_p[2]), jnp.stack(outs_p[3]),
            jnp.stack(outs_s[0]), jnp.stack(outs_s[1]), jnp.stack(outs_s[2]), jnp.stack(outs_s[3]))
```

```python
import functools
import math

import jax
import jax.numpy as jnp
from jax import lax
from jax.experimental import pallas as pl
from jax.experimental.pallas import tpu as pltpu

HA, DA = 4, 128
DA_HALF = DA // 2
HB, DB = 4, 128
MOBA_BLOCK = 256
MOBA_TOPK = 3
HC, DK, DV = 4, 128, 128
CONV_W = 4
GDN_CHUNK = 64
C_CONV = HC * (2 * DK + DV)
N_BRANCH = 3
BRANCH_W = HA * DA
N_GROUPS = 4
EXPERTS_PER_GROUP = 8
N_EXPERTS = N_GROUPS * EXPERTS_PER_GROUP
D_EXPERT = 256
RMS_EPS = 1e-6

LANES = 128
SUBLANES = 8
VMEM_LIMIT_BYTES = 56 * 1024 * 1024

NEG = -1e30
F32 = jnp.float32
BF16 = jnp.bfloat16

_SPLITS = (HA * DA, HA * DA, HA * DA, HB * DB, HB * DB, HB * DB, C_CONV, HC * DV, HC, HC)
_OFF = [0]
for _s in _SPLITS:
    _OFF.append(_OFF[-1] + _s)
N_MIX_COLS = _OFF[-1]
N_MIX_PAD = -(-N_MIX_COLS // LANES) * LANES
BA_COL = _OFF[8]

ROW_TILE = 512
ATTN_TQ = 256
MOE_TILE = 256
PAGES_PER_STEP = 8
MEAN_PAGES_PER_STEP = 16


def _alibi_slopes():
    n = HA + HB
    s = [2.0 ** (-8.0 * i / n) for i in range(1, n + 1)]
    return s[0::2], s[1::2]


def _cparams(semantics):
    return pltpu.CompilerParams(dimension_semantics=semantics, vmem_limit_bytes=VMEM_LIMIT_BYTES)


def _row_tile(n, pref):
    return pref if n % pref == 0 else n


def _resident(shape):
    zeros = (0,) * len(shape)
    return pl.BlockSpec(shape, lambda *_: zeros, pipeline_mode=pl.Buffered(1))


def _rms(x, gain_row):
    return x * lax.rsqrt(jnp.mean(x * x, axis=-1, keepdims=True) + RMS_EPS) * gain_row


def _dot(a, b, **kw):
    return jnp.dot(a, b, preferred_element_type=F32, **kw)


def _dot_nt(a, b, **kw):
    return lax.dot_general(a, b, (((1,), (1,)), ((), ())), preferred_element_type=F32, **kw)


def _dot_tn(a, b, **kw):
    return lax.dot_general(a, b, (((0,), (0,)), ((), ())), preferred_element_type=F32, **kw)


def _inproj_kernel(x_ref, g_ref, w_ref, qa_ref, kva_ref, qb_ref, kvb_ref, cqkv_ref, cg_ref, ba_ref):
    h = _rms(x_ref[...], g_ref[...]).astype(BF16)

    def mm(lo, hi):
        return _dot(h, w_ref[:, lo:hi])

    qa_ref[...] = mm(_OFF[0], _OFF[1]).astype(qa_ref.dtype)
    kva_ref[...] = mm(_OFF[1], _OFF[3])
    qb_ref[...] = mm(_OFF[3], _OFF[4]).astype(qb_ref.dtype)
    kvb_ref[...] = mm(_OFF[4], _OFF[6])
    cqkv_ref[...] = mm(_OFF[6], _OFF[7])
    cg_ref[...] = mm(_OFF[7], _OFF[8])
    ba_ref[...] = mm(BA_COL, N_MIX_PAD)


def _inproj(x2d, gain_row, w_mix, q_dtype):
    n, d = x2d.shape
    tm = _row_tile(n, ROW_TILE)
    widths = (HA * DA, 2 * HA * DA, HB * DB, 2 * HB * DB, C_CONV, HC * DV, N_MIX_PAD - BA_COL)
    dtypes = (q_dtype, F32, q_dtype, F32, F32, F32, F32)
    return pl.pallas_call(
        _inproj_kernel,
        grid=(n // tm,),
        in_specs=[pl.BlockSpec((tm, d), lambda i: (i, 0)), _resident((1, d)), _resident(w_mix.shape)],
        out_specs=[pl.BlockSpec((tm, w), lambda i: (i, 0)) for w in widths],
        out_shape=[jax.ShapeDtypeStruct((n, w), dt) for w, dt in zip(widths, dtypes)],
        compiler_params=_cparams(("parallel",)),
        name="inproj",
    )(x2d, gain_row, w_mix)


KV_CHUNKS = 2 * HA * DA // LANES


def _store_kv_rows(out_ref, kv, tm):
    for c in range(KV_CHUNKS):
        out_ref[pl.ds(c, tm, stride=KV_CHUNKS), :] = kv[:, c * LANES:(c + 1) * LANES]


def _inproj_prompt_kernel(x_ref, g_ref, w_ref, qa_ref, kva_ref, kva16_ref, qb_ref, kvb_ref, kvb16_ref, means_ref,
                          cqkv_ref, cg_ref, ba_ref, *, tm):
    h = _rms(x_ref[...], g_ref[...]).astype(BF16)

    def mm(lo, hi):
        return _dot(h, w_ref[:, lo:hi])

    qa_ref[...] = mm(_OFF[0], _OFF[1]).astype(BF16)
    kva = mm(_OFF[1], _OFF[3])
    kva16_ref[...] = kva.astype(BF16)
    _store_kv_rows(kva_ref, kva, tm)
    qb_ref[...] = mm(_OFF[3], _OFF[4]).astype(BF16)
    kvb = mm(_OFF[4], _OFF[6])
    kvb16_ref[...] = kvb.astype(BF16)
    _store_kv_rows(kvb_ref, kvb, tm)
    for jb in range(tm // MOBA_BLOCK):
        blk = kvb[jb * MOBA_BLOCK:(jb + 1) * MOBA_BLOCK, :HB * DB]
        means_ref[jb:jb + 1, :] = jnp.sum(blk, axis=0, keepdims=True) * (1.0 / MOBA_BLOCK)
    cqkv_ref[...] = mm(_OFF[6], _OFF[7])
    cg_ref[...] = mm(_OFF[7], _OFF[8])
    ba_ref[...] = mm(BA_COL, N_MIX_PAD)


def _inproj_prompt(x2d, gain_row, w_mix):
    n, d = x2d.shape
    tm = ROW_TILE
    assert n % tm == 0 and tm % MOBA_BLOCK == 0 and HA * DA == HB * DB
    row = lambda i: (i, 0)
    kvw = 2 * HA * DA
    out = [
        (pl.BlockSpec((tm, HA * DA), row), jax.ShapeDtypeStruct((n, HA * DA), BF16)),
        (pl.BlockSpec((tm * KV_CHUNKS, LANES), row), jax.ShapeDtypeStruct((n * KV_CHUNKS, LANES), F32)),
        (pl.BlockSpec((tm, kvw), row), jax.ShapeDtypeStruct((n, kvw), BF16)),
        (pl.BlockSpec((tm, HB * DB), row), jax.ShapeDtypeStruct((n, HB * DB), BF16)),
        (pl.BlockSpec((tm * KV_CHUNKS, LANES), row), jax.ShapeDtypeStruct((n * KV_CHUNKS, LANES), F32)),
        (pl.BlockSpec((tm, kvw), row), jax.ShapeDtypeStruct((n, kvw), BF16)),
        (pl.BlockSpec((None, tm // MOBA_BLOCK, HB * DB), lambda i: (i, 0, 0)),
         jax.ShapeDtypeStruct((n // tm, tm // MOBA_BLOCK, HB * DB), F32)),
        (pl.BlockSpec((tm, C_CONV), row), jax.ShapeDtypeStruct((n, C_CONV), F32)),
        (pl.BlockSpec((tm, HC * DV), row), jax.ShapeDtypeStruct((n, HC * DV), F32)),
        (pl.BlockSpec((tm, N_MIX_PAD - BA_COL), row), jax.ShapeDtypeStruct((n, N_MIX_PAD - BA_COL), F32)),
    ]
    return pl.pallas_call(
        functools.partial(_inproj_prompt_kernel, tm=tm),
        grid=(n // tm,),
        in_specs=[pl.BlockSpec((tm, d), row), _resident((1, d)), _resident(w_mix.shape)],
        out_specs=[o[0] for o in out],
        out_shape=[o[1] for o in out],
        compiler_params=_cparams(("parallel",)),
        name="inproj_prompt",
    )(x2d, gain_row, w_mix)


def _diff_lambda(dl, lam_init):
    s1 = jnp.sum(dl[0:1, :] * dl[1:2, :], axis=1, keepdims=True)
    s2 = jnp.sum(dl[2:3, :] * dl[3:4, :], axis=1, keepdims=True)
    return jnp.exp(s1) - jnp.exp(s2) + lam_init


def _split_halves(q):
    lane = lax.broadcasted_iota(jnp.int32, q.shape, 1)
    zero = jnp.zeros_like(q)
    return jnp.concatenate([jnp.where(lane < DA_HALF, q, zero), jnp.where(lane >= DA_HALF, q, zero)], axis=0)


def _sample_softmax_pv(s_sc, v_sc, qs, kvn_ref, n_heads, tok, slopes, scale, ts, lam=None):
    heads = range(n_heads)
    d = qs[0].shape[1]
    kn = [_bf16_round(kvn_ref[:, h * d:(h + 1) * d]) for h in heads]
    vn = [_bf16_round(kvn_ref[:, (n_heads + h) * d:(n_heads + h + 1) * d]) for h in heads]
    fresh = [[jnp.where(tok >= j, jnp.sum(qs[h] * kn[h][j:j + 1, :], axis=-1, keepdims=True) * scale
                        - slopes[h] * (tok - j).astype(F32), NEG) for j in range(ts)] for h in heads]
    s_all = [s_sc[h] for h in heads]
    m = [functools.reduce(jnp.maximum, fresh[h] + [jnp.max(s_all[h], axis=-1, keepdims=True)]) for h in heads]
    p = [jnp.exp(s_all[h] - m[h]) for h in heads]
    pf = [[jnp.exp(f - m[h]) for f in fresh[h]] for h in heads]
    inv = [1.0 / functools.reduce(jnp.add, pf[h] + [jnp.sum(p[h], axis=-1, keepdims=True)]) for h in heads]
    p = [p[h] * inv[h] for h in heads]
    pf = [[x * inv[h] for x in pf[h]] for h in heads]
    if lam is not None:
        p = [x[:ts] - lam * x[ts:] for x in p]
        pf = [[x[:ts] - lam * x[ts:] for x in pf[h]] for h in heads]
        p = [jnp.concatenate([x, x], axis=0) for x in p]
    out = [_dot(p[h].astype(BF16), v_sc[h])[:ts] for h in heads]
    return [functools.reduce(jnp.add, [out[h]] + [_bf16_round(x[:ts]) * vn[h][j:j + 1, :]
                                                   for j, x in enumerate(pf[h])]) for h in heads]


def _bf16_round(x):
    return x.astype(BF16).astype(F32)


def _top_blocks(gate, eligible):
    lane = lax.broadcasted_iota(jnp.int32, gate.shape, 1).astype(F32)
    g = jnp.where(eligible, gate, NEG)
    sel = jnp.zeros(gate.shape, F32)
    for _ in range(MOBA_TOPK):
        mx = jnp.max(g, axis=-1, keepdims=True)
        first = jnp.min(jnp.where(g == mx, lane, float(LANES)), axis=-1, keepdims=True)
        hit = (lane == first) & (mx > 0.5 * NEG)
        sel = jnp.where(hit, 1.0, sel)
        g = jnp.where(hit, NEG, g)
    return sel


def _lane_column(mat, n):
    lane = lax.broadcasted_iota(jnp.int32, mat.shape, 1)
    return jnp.sum(jnp.where(lane == n, mat, 0.0), axis=-1, keepdims=True)


def _tile_offsets(rows, width, tq):
    row = lax.broadcasted_iota(jnp.int32, (rows, width), 0)
    col = lax.broadcasted_iota(jnp.int32, (rows, width), 1)
    return (row & (tq - 1)) - col


ATTN_ROW_GROUP = 128


def _causal_softmax_pv(qs, kbf, vbf, bias, width, tq, scale=None, block_off=None, lam=None):
    rg = min(ATTN_ROW_GROUP, tq)
    assert tq % rg == 0
    groups = [slice(i * rg, (i + 1) * rg) for i in range(tq // rg)]
    units = [(g, q) for g in groups for q in qs]
    past = width - tq
    ss = [_dot_nt(q[g], kbf[0:width, :]) for g, q in units]
    if scale is not None:
        ss = [s * scale for s in ss]
    ss = [s + bias[g, 0:width] for s, (g, _) in zip(ss, units)]
    masked = []
    for s, (g, _) in zip(ss, units):
        pieces = []
        for j in range(past // tq):
            piece = s[:, j * tq:(j + 1) * tq]
            pieces.append(piece if block_off is None else piece + block_off[g, j:j + 1])
        own = jnp.where(_tile_offsets(rg, tq, tq) + g.start >= 0, s[:, past:], NEG)
        masked.append(jnp.concatenate(pieces + [own], axis=1) if pieces else own)
    ps = [jnp.exp(s - jnp.max(s, axis=-1, keepdims=True)) for s in masked]
    ps = [p * (1.0 / jnp.sum(p, axis=-1, keepdims=True)) for p in ps]
    if len(qs) == 2:
        ps = [ps[2 * i] - lam * ps[2 * i + 1] for i in range(len(groups))]
    os = [_dot(p.astype(BF16), vbf[0:width, :]) for p in ps]
    return jnp.concatenate(os, axis=0) if len(os) > 1 else os[0]


def _diff_prompt_kernel(q_ref, kbf, vbf, dl_ref, gain_ref, slope_ref, o_ref, bias, *, tq, nq, lam_init):
    qi = pl.program_id(2)

    @pl.when(qi == 0)
    def _():
        bias[...] = -slope_ref[:, 0:1] * _tile_offsets(tq, bias.shape[1], tq).astype(F32)

    q2 = _split_halves(q_ref[...] * (DA_HALF ** -0.5))
    lam = _diff_lambda(dl_ref[...], lam_init)

    for v in range(nq):
        @pl.when(qi == v)
        def _(v=v):
            out = _causal_softmax_pv([q2[:tq], q2[tq:]], kbf, vbf, bias, (v + 1) * tq, tq, lam=lam)
            o_ref[...] = (_rms(out, gain_ref[...]) * (1.0 - lam_init)).astype(o_ref.dtype)


def _diff_attn_prompt(q, kv, dl, gain_row, slopes, lam_init, b, t):
    n = b * t
    tq = ATTN_TQ
    assert t % tq == 0 and tq & (tq - 1) == 0
    nq = t // tq
    assert nq <= 16
    kern = functools.partial(_diff_prompt_kernel, tq=tq, nq=nq, lam_init=lam_init)
    return pl.pallas_call(
        kern,
        grid=(b, HA, nq),
        in_specs=[
            pl.BlockSpec((tq, DA), lambda bi, h, qi: (bi * nq + qi, h)),
            pl.BlockSpec((t, DA), lambda bi, h, qi: (bi, h)),
            pl.BlockSpec((t, DA), lambda bi, h, qi: (bi, HA + h)),
            pl.BlockSpec((4, DA_HALF), lambda bi, h, qi: (0, 0)),
            pl.BlockSpec((1, DA), lambda bi, h, qi: (0, 0)),
            pl.BlockSpec((None, 1, LANES), lambda bi, h, qi: (h, 0, 0)),
        ],
        out_specs=pl.BlockSpec((tq, DA), lambda bi, h, qi: (bi * nq + qi, h)),
        out_shape=jax.ShapeDtypeStruct((n, HA * DA), BF16),
        scratch_shapes=[pltpu.VMEM((tq, t), F32)],
        compiler_params=_cparams(("parallel", "parallel", "arbitrary")),
        name="diff_attn_prompt",
    )(q, kv, kv, dl, gain_row, slopes)


def _moba_prompt_kernel(q_ref, kbf, vbf, means_ref, slope_ref, o_ref, means, bias, *, tq, nb):
    qi = pl.program_id(2)
    slope = slope_ref[:, 0:1]

    @pl.when(qi == 0)
    def _():
        bias[...] = -slope * _tile_offsets(tq, bias.shape[1], tq).astype(F32)
        means[...] = jnp.zeros(means.shape, F32)
        means[0:nb, :] = means_ref[...]

    q = q_ref[...]
    scale = DB ** -0.5
    gate = _dot_nt(q, means[...].astype(BF16))
    lane = lax.broadcasted_iota(jnp.int32, gate.shape, 1)
    off = jnp.where(_top_blocks(gate, lane < qi) > 0.5, 0.0, NEG)

    for v in range(nb):
        @pl.when(qi == v)
        def _(v=v):
            o_ref[...] = _causal_softmax_pv([q], kbf, vbf, bias, (v + 1) * tq, tq, scale, off).astype(o_ref.dtype)


def _moba_attn_prompt(q, kv, means, slopes, b, t):
    n = b * t
    tq = ATTN_TQ
    assert tq == MOBA_BLOCK and t % tq == 0 and tq & (tq - 1) == 0
    nq = t // tq
    assert nq <= 16
    kern = functools.partial(_moba_prompt_kernel, tq=tq, nb=nq)
    return pl.pallas_call(
        kern,
        grid=(b, HB, nq),
        in_specs=[
            pl.BlockSpec((tq, DB), lambda bi, h, qi: (bi * nq + qi, h)),
            pl.BlockSpec((t, DB), lambda bi, h, qi: (bi, h)),
            pl.BlockSpec((t, DB), lambda bi, h, qi: (bi, HB + h)),
            pl.BlockSpec((None, nq, DB), lambda bi, h, qi: (bi, 0, h)),
            pl.BlockSpec((None, 1, LANES), lambda bi, h, qi: (h, 0, 0)),
        ],
        out_specs=pl.BlockSpec((tq, DB), lambda bi, h, qi: (bi * nq + qi, h)),
        out_shape=jax.ShapeDtypeStruct((n, HB * DB), BF16),
        scratch_shapes=[pltpu.VMEM((LANES, DB), F32), pltpu.VMEM((tq, t), F32)],
        compiler_params=_cparams(("parallel", "parallel", "arbitrary")),
        name="moba_attn_prompt",
    )(q, kv, kv, means, slopes)


def _head_rows(pages, kv, h, n_heads, page):
    parts = [pg[pl.ds(kv * n_heads + h, page, stride=2 * n_heads), :] for pg in pages]
    return jnp.concatenate(parts, axis=0).astype(BF16)


def _diff_sample_kernel(pt_ref, q_ref, kvn_ref, dl_ref, gain_ref, *rest, n_in, ts, page, past_len, lam_init,
                        slopes):
    del pt_ref
    pages = rest[:n_in]
    o_ref = rest[n_in]
    s_sc, v_sc = rest[n_in + 1:]
    i = pl.program_id(1)
    rows = 2 * ts
    nkeys = n_in * page
    row1 = lax.broadcasted_iota(jnp.int32, (rows, 1), 0)
    tok1 = jnp.where(row1 < ts, row1, row1 - ts)
    lane = lax.broadcasted_iota(jnp.int32, (rows, nkeys), 1)
    dist = ((past_len + tok1) - (i * nkeys + lane)).astype(F32)
    q_all = q_ref[...]
    q2 = [_split_halves(q_all[:, h * DA:(h + 1) * DA] * (DA_HALF ** -0.5)).astype(BF16) for h in range(HA)]

    off = pl.multiple_of(i * nkeys, nkeys)
    heads = range(HA)
    ss = [_dot_nt(q2[h], _head_rows(pages, 0, h, HA, page)) - slopes[h] * dist for h in heads]
    for h in heads:
        s_sc[h, :, pl.ds(off, nkeys)] = ss[h]
        v_sc[h, pl.ds(off, nkeys), :] = _head_rows(pages, 1, h, HA, page)

    @pl.when(i == pl.num_programs(1) - 1)
    def _():
        lam = _diff_lambda(dl_ref[...], lam_init)
        outs = _sample_softmax_pv(s_sc, v_sc, [q2[h].astype(F32) for h in heads], kvn_ref, HA, tok1, slopes, 1.0,
                                  ts, lam)
        for h in heads:
            o_ref[:, h * DA:(h + 1) * DA] = _rms(outs[h], gain_ref[...]) * (1.0 - lam_init)


def _page_specs(layer, n_pages, n_in, rows):
    def spec(j):
        return pl.BlockSpec((None, None, rows, LANES),
                            lambda b, i, pt: (layer, pt[b * n_pages + i * n_in + j], 0, 0))
    return [spec(j) for j in range(n_in)]


def _diff_attn_sample(q3, kvn3, cache4, pt_flat, dl, gain_row, slopes, lam_init, layer, n_pages, page):
    bs, ts, _ = q3.shape
    assert 2 * ts == SUBLANES
    n_in = min(PAGES_PER_STEP, n_pages)
    assert n_pages % n_in == 0
    rows = cache4.shape[2]
    kern = functools.partial(_diff_sample_kernel, n_in=n_in, ts=ts, page=page, past_len=n_pages * page,
                             lam_init=lam_init, slopes=slopes)
    grid_spec = pltpu.PrefetchScalarGridSpec(
        num_scalar_prefetch=1,
        grid=(bs, n_pages // n_in),
        in_specs=[
            pl.BlockSpec((None, ts, HA * DA), lambda b, i, pt: (b, 0, 0)),
            pl.BlockSpec((None, ts, 2 * HA * DA), lambda b, i, pt: (b, 0, 0)),
            pl.BlockSpec((4, DA_HALF), lambda b, i, pt: (0, 0)),
            pl.BlockSpec((1, DA), lambda b, i, pt: (0, 0)),
        ] + _page_specs(layer, n_pages, n_in, rows),
        out_specs=pl.BlockSpec((None, ts, HA * DA), lambda b, i, pt: (b, 0, 0)),
        scratch_shapes=[pltpu.VMEM((HA, 2 * ts, n_pages * page), F32), pltpu.VMEM((HA, n_pages * page, DA), BF16)],
    )
    return pl.pallas_call(
        kern,
        grid_spec=grid_spec,
        out_shape=jax.ShapeDtypeStruct((bs, ts, HA * DA), F32),
        compiler_params=_cparams(("parallel", "arbitrary")),
        name="diff_attn_sample",
    )(pt_flat, q3, kvn3, dl, gain_row, *([cache4] * n_in))


def _moba_means_kernel(pt_ref, *rest, n_in, pages_per_block, page):
    del pt_ref
    pages = rest[:n_in]
    o_ref = rest[n_in]
    inv = 1.0 / (pages_per_block * page)
    for jb in range(n_in // pages_per_block):
        tot = None
        for j in range(pages_per_block):
            part = jnp.sum(pages[jb * pages_per_block + j][...], axis=0)
            tot = part if tot is None else tot + part
        tot = tot * inv
        for h in range(HB):
            o_ref[h, jb:jb + 1, :] = tot[h:h + 1, :]


def _moba_means_sample(cache6, pt_flat, layer, bs, n_pages, page):
    assert MOBA_BLOCK % page == 0
    ppb = MOBA_BLOCK // page
    n_in = min(MEAN_PAGES_PER_STEP, n_pages)
    assert n_pages % n_in == 0 and n_in % ppb == 0
    nb_step = n_in // ppb
    nb = n_pages // ppb
    assert nb_step % SUBLANES == 0 or nb_step == nb

    def spec(j):
        return pl.BlockSpec((None, None, page, None, HB, DB),
                            lambda b, i, pt: (layer, pt[b * n_pages + i * n_in + j], 0, 0, 0, 0))

    kern = functools.partial(_moba_means_kernel, n_in=n_in, pages_per_block=ppb, page=page)
    grid_spec = pltpu.PrefetchScalarGridSpec(
        num_scalar_prefetch=1,
        grid=(bs, n_pages // n_in),
        in_specs=[spec(j) for j in range(n_in)],
        out_specs=pl.BlockSpec((None, HB, nb_step, DB), lambda b, i, pt: (b, 0, i, 0)),
    )
    return pl.pallas_call(
        kern,
        grid_spec=grid_spec,
        out_shape=jax.ShapeDtypeStruct((bs, HB, nb, DB), F32),
        compiler_params=_cparams(("parallel", "arbitrary")),
        name="moba_means_sample",
    )(pt_flat, *([cache6] * n_in))


def _moba_sample_kernel(pt_ref, q_ref, kvn_ref, means_ref, *rest, n_in, ts, page, past_len, slopes):
    del pt_ref
    pages = rest[:n_in]
    o_ref = rest[n_in]
    sel_s, s_sc, v_sc = rest[n_in + 1:]
    i = pl.program_id(1)
    rows = SUBLANES
    nkeys = n_in * page
    row1 = lax.broadcasted_iota(jnp.int32, (rows, 1), 0)
    tok1 = jnp.where(row1 < ts, row1, row1 - ts)
    lane = lax.broadcasted_iota(jnp.int32, (rows, nkeys), 1)
    dist = ((past_len + tok1) - (i * nkeys + lane)).astype(F32)
    scale = DB ** -0.5
    blocks_per_step = nkeys // MOBA_BLOCK
    nb_past = past_len // MOBA_BLOCK
    q_all = q_ref[...]
    q8 = [jnp.concatenate([q_all[:, h * DB:(h + 1) * DB]] * 2, axis=0).astype(BF16) for h in range(HB)]

    @pl.when(i == 0)
    def _():
        for h in range(HB):
            gate = _dot_nt(q8[h], means_ref[h].astype(BF16))
            glane = lax.broadcasted_iota(jnp.int32, gate.shape, 1)
            sel_s[h] = _top_blocks(gate, glane < nb_past)

    off = pl.multiple_of(i * nkeys, nkeys)
    heads = range(HB)
    picked = [jnp.concatenate(
        [jnp.broadcast_to(_lane_column(sel_s[h], i * blocks_per_step + jb), (rows, MOBA_BLOCK))
         for jb in range(blocks_per_step)], axis=1) > 0.5 for h in heads]
    ss = [_dot_nt(q8[h], _head_rows(pages, 0, h, HB, page)) for h in heads]
    ss = [jnp.where(picked[h], ss[h] * scale - slopes[h] * dist, NEG) for h in heads]
    for h in heads:
        s_sc[h, :, pl.ds(off, nkeys)] = ss[h]
        v_sc[h, pl.ds(off, nkeys), :] = _head_rows(pages, 1, h, HB, page)

    @pl.when(i == pl.num_programs(1) - 1)
    def _():
        outs = _sample_softmax_pv(s_sc, v_sc, [q8[h].astype(F32) for h in heads], kvn_ref, HB, tok1, slopes, scale, ts)
        for h in heads:
            o_ref[:, h * DB:(h + 1) * DB] = outs[h]


def _moba_attn_sample(q3, kvn3, means_pad, cache4, pt_flat, slopes, layer, n_pages, page):
    bs, ts, _ = q3.shape
    assert 2 * ts == SUBLANES
    past_len = n_pages * page
    assert past_len % MOBA_BLOCK == 0 and ts <= MOBA_BLOCK and past_len // MOBA_BLOCK <= LANES
    n_in = min(PAGES_PER_STEP, n_pages)
    assert n_pages % n_in == 0 and (n_in * page) % MOBA_BLOCK == 0
    rows = cache4.shape[2]
    kern = functools.partial(_moba_sample_kernel, n_in=n_in, ts=ts, page=page, past_len=past_len, slopes=slopes)
    grid_spec = pltpu.PrefetchScalarGridSpec(
        num_scalar_prefetch=1,
        grid=(bs, n_pages // n_in),
        in_specs=[
            pl.BlockSpec((None, ts, HB * DB), lambda b, i, pt: (b, 0, 0)),
            pl.BlockSpec((None, ts, 2 * HB * DB), lambda b, i, pt: (b, 0, 0)),
            pl.BlockSpec((None, HB, LANES, DB), lambda b, i, pt: (b, 0, 0, 0)),
        ] + _page_specs(layer, n_pages, n_in, rows),
        out_specs=pl.BlockSpec((None, ts, HB * DB), lambda b, i, pt: (b, 0, 0)),
        scratch_shapes=[pltpu.VMEM((HB, SUBLANES, LANES), F32), pltpu.VMEM((HB, SUBLANES, past_len), F32),
                        pltpu.VMEM((HB, past_len, DB), BF16)],
    )
    return pl.pallas_call(
        kern,
        grid_spec=grid_spec,
        out_shape=jax.ShapeDtypeStruct((bs, ts, HB * DB), F32),
        compiler_params=_cparams(("parallel", "arbitrary")),
        name="moba_attn_sample",
    )(pt_flat, q3, kvn3, means_pad, *([cache4] * n_in))


def _split_bf16(x, parts):
    out = []
    for _ in range(parts):
        hi = x.astype(BF16)
        out.append(hi)
        x = x - hi.astype(F32)
    return out


def _dot3(a, b):
    a_hi, a_lo = _split_bf16(a, 2)
    b_hi, b_lo = _split_bf16(b, 2)
    return _dot(a_hi, b_hi) + (_dot(a_hi, b_lo) + _dot(a_lo, b_hi))


def _unit_lower_inverses(nmats):
    c = nmats[0].shape[0]
    row = lax.broadcasted_iota(jnp.int32, (c, c), 0)
    col = lax.broadcasted_iota(jnp.int32, (c, c), 1)
    eye = (row == col).astype(F32)
    shift = 3
    same = lax.shift_right_logical(row, shift) == lax.shift_right_logical(col, shift)
    n0 = [jnp.where(same, n, 0.0) for n in nmats]
    n2 = [_dot3(a, a) for a in n0]
    n4 = [_dot3(a, a) for a in n2]
    inv = [_dot3(eye - a, eye + b) for a, b in zip(n0, n2)]
    inv = [_dot3(a, eye + b) for a, b in zip(inv, n4)]
    while (1 << shift) < c:
        shift += 1
        same2 = lax.shift_right_logical(row, shift) == lax.shift_right_logical(col, shift)
        grow = same2 & jnp.logical_not(same)
        left = [_dot3(a, jnp.where(grow, n, 0.0)) for a, n in zip(inv, nmats)]
        inv = [a - _dot3(b, a) for a, b in zip(inv, left)]
        same = same2
    return inv


def _gdn_prep_kernel(cqkv_ref, halo_ref, ba_ref, prev_ref, cw_ref, alog_ref, dtb_ref,
                     u_ref, w_ref, qd_ref, kd_ref, qk_ref, egl_ref, xbuf, *, chunk, n_chunks, t_valid):
    c = pl.program_id(1)
    rows = n_chunks * chunk
    halo = SUBLANES
    xbuf[0:halo, :] = jnp.where(c == 0, prev_ref[...], halo_ref[...])
    xbuf[halo:halo + rows, :] = cqkv_ref[...]
    conv = cw_ref[0:1, :] * xbuf[halo - 3:halo - 3 + rows, :]
    for j in range(1, CONV_W):
        conv = conv + cw_ref[j:j + 1, :] * xbuf[halo - 3 + j:halo - 3 + j + rows, :]
    conv = conv * jax.nn.sigmoid(conv)

    rowc = lax.broadcasted_iota(jnp.int32, (rows, 1), 0)
    valid = (c * rows + rowc) < t_valid
    ba = ba_ref[...]
    beta_all = jnp.where(valid, jax.nn.sigmoid(ba), 0.0)
    g_all = jnp.where(valid, -jnp.exp(alog_ref[...]) * jax.nn.softplus(ba + dtb_ref[...]), 0.0)
    ri = lax.broadcasted_iota(jnp.int32, (chunk, chunk), 0)
    ci = lax.broadcasted_iota(jnp.int32, (chunk, chunk), 1)
    tril16 = (ri >= ci).astype(BF16)
    eye16 = (ri == ci).astype(BF16)

    probs = []
    for g in range(n_chunks):
        r0 = g * chunk
        gc_all = functools.reduce(jnp.add, [_dot(tril16, p) for p in _split_bf16(g_all[r0:r0 + chunk], 3)])
        gc_t = functools.reduce(jnp.add, [_dot_tn(p, eye16) for p in _split_bf16(gc_all, 3)])
        egl_ref[r0:r0 + chunk, :] = jnp.broadcast_to(jnp.exp(gc_all[chunk - 1:chunk, :]), (chunk, LANES))
        for h in range(HC):
            qh = conv[r0:r0 + chunk, h * DK:(h + 1) * DK]
            kh = conv[r0:r0 + chunk, (HC + h) * DK:(HC + h + 1) * DK]
            vh = conv[r0:r0 + chunk, 2 * HC * DK + h * DV:2 * HC * DK + (h + 1) * DV]
            qh = qh * lax.rsqrt(jnp.sum(qh * qh, axis=-1, keepdims=True) + 1e-6) * (DK ** -0.5)
            kh = kh * lax.rsqrt(jnp.sum(kh * kh, axis=-1, keepdims=True) + 1e-6)
            beta = _lane_column(beta_all[r0:r0 + chunk], h)
            gc = _lane_column(gc_all, HC + h)
            gc_row = gc_t[HC + h:HC + h + 1, :]
            probs.append(dict(r0=r0, h=h, qh=qh, kh=kh, vh=vh, beta=beta, gc=gc, diff=gc - gc_row,
                              gc_last=gc_row[:, chunk - 1:chunk], kbeta=kh * beta, k16=kh.astype(BF16)))
    nmats = [_dot_nt(p["kbeta"].astype(BF16), p["k16"]) * jnp.exp(jnp.where(ri > ci, p["diff"], NEG)) for p in probs]
    qks = [_dot_nt(p["qh"].astype(BF16), p["k16"]) * jnp.exp(jnp.where(ri >= ci, p["diff"], NEG)) for p in probs]
    invs = _unit_lower_inverses(nmats)
    uws = []
    for p, inv in zip(probs, invs):
        egc = jnp.exp(p["gc"])
        rhs = jnp.concatenate([(p["vh"] * p["beta"]).astype(BF16), (p["kbeta"] * egc).astype(BF16)], axis=1)
        uws.append(_dot(inv.astype(BF16), rhs))
        p["egc"] = egc
    for p, uw, qk in zip(probs, uws, qks):
        rs = slice(p["r0"], p["r0"] + chunk)
        hs = slice(p["h"] * DK, (p["h"] + 1) * DK)
        u_ref[rs, hs] = uw[:, :DV]
        w_ref[rs, hs] = uw[:, DV:].astype(BF16)
        qk_ref[rs, p["h"] * chunk:(p["h"] + 1) * chunk] = qk.astype(BF16)
        qd_ref[rs, hs] = (p["qh"] * p["egc"]).astype(BF16)
        kd_ref[rs, hs] = (p["kh"] * jnp.exp(p["gc_last"] - p["gc"])).astype(BF16)


def _gdn_scan_kernel(u_ref, w_ref, qd_ref, kd_ref, qk_ref, egl_ref, cg_ref, s0_ref, gain_ref, o_ref, s_ref,
                     *, n_seq, chunk):
    c = pl.program_id(1)

    @pl.when(c == 0)
    def _():
        s_ref[...] = s0_ref[...]

    chains = [(bb, h, slice(h * DK, (h + 1) * DK)) for bb in range(n_seq) for h in range(HC)]
    s_old = [s_ref[bb, h] for bb, h, _ in chains]
    s16 = [s.astype(BF16) for s in s_old]
    ws = [_dot(jnp.concatenate([w_ref[bb, :, hs], qd_ref[bb, :, hs]], axis=0), s)
          for (bb, _, hs), s in zip(chains, s16)]
    v16 = [(u_ref[bb, :, hs] - x[:chunk]).astype(BF16) for (bb, _, hs), x in zip(chains, ws)]
    o = [x[chunk:] + _dot(qk_ref[bb, :, h * chunk:(h + 1) * chunk], v) for (bb, h, _), x, v in zip(chains, ws, v16)]
    kv = [_dot_tn(kd_ref[bb, :, hs], v) for (bb, _, hs), v in zip(chains, v16)]
    for (bb, h, hs), s, upd, out in zip(chains, s_old, kv, o):
        decay = _lane_column(egl_ref[bb, 0:1, :], HC + h)
        s_ref[bb, h] = s * decay + upd
        gate = cg_ref[bb, :, hs]
        o_ref[bb, :, hs] = (_rms(out, gain_ref[...]) * (gate * jax.nn.sigmoid(gate))).astype(o_ref.dtype)


def _gdn(cqkv, cg, ba, prev8, s0, conv_w8, alog_row, dtb_row, gain_row, b, t, t_valid):
    chunk = GDN_CHUNK
    assert t % chunk == 0 and DK == DV
    nc = t // chunk
    n = b * t
    g = 2 if nc % 2 == 0 else 1
    rows = g * chunk
    steps = nc // g
    hb = SUBLANES
    prep = functools.partial(_gdn_prep_kernel, chunk=chunk, n_chunks=g, t_valid=t_valid)
    blk = lambda width: pl.BlockSpec((rows, width), lambda bi, c: (bi * steps + c, 0))
    u, w, qd, kd, qk, egl = pl.pallas_call(
        prep,
        grid=(b, steps),
        in_specs=[
            blk(C_CONV),
            pl.BlockSpec((hb, C_CONV), lambda bi, c: (jnp.maximum((bi * t + c * rows) // hb - 1, 0), 0)),
            blk(LANES),
            pl.BlockSpec((None, hb, C_CONV), lambda bi, c: (bi, 0, 0)),
            pl.BlockSpec((SUBLANES, C_CONV), lambda bi, c: (0, 0)),
            pl.BlockSpec((1, LANES), lambda bi, c: (0, 0)),
            pl.BlockSpec((1, LANES), lambda bi, c: (0, 0)),
        ],
        out_specs=[blk(HC * DV), blk(HC * DK), blk(HC * DK), blk(HC * DK), blk(HC * chunk), blk(LANES)],
        out_shape=[jax.ShapeDtypeStruct((n, HC * DV), F32), jax.ShapeDtypeStruct((n, HC * DK), BF16),
                   jax.ShapeDtypeStruct((n, HC * DK), BF16), jax.ShapeDtypeStruct((n, HC * DK), BF16),
                   jax.ShapeDtypeStruct((n, HC * chunk), BF16), jax.ShapeDtypeStruct((n, LANES), F32)],
        scratch_shapes=[pltpu.VMEM((rows + hb, C_CONV), F32)],
        compiler_params=_cparams(("parallel", "parallel")),
        name="gdn_prep",
    )(cqkv, cqkv, ba, prev8, conv_w8, alog_row, dtb_row)

    n_seq = 4 if b % 4 == 0 else (2 if b % 2 == 0 else 1)
    scan = functools.partial(_gdn_scan_kernel, n_seq=n_seq, chunk=chunk)
    seq = lambda width: pl.BlockSpec((n_seq, chunk, width), lambda bg, c: (bg, c, 0))
    state = pl.BlockSpec((n_seq, HC, DK, DV), lambda bg, c: (bg, 0, 0, 0))
    as3 = lambda a: a.reshape(b, t, a.shape[-1])
    out, s_new = pl.pallas_call(
        scan,
        grid=(b // n_seq, nc),
        in_specs=[seq(HC * DV), seq(HC * DK), seq(HC * DK), seq(HC * DK), seq(HC * chunk), seq(LANES),
                  seq(HC * DV), state, pl.BlockSpec((1, DV), lambda bg, c: (0, 0))],
        out_specs=[seq(HC * DV), state],
        out_shape=[jax.ShapeDtypeStruct((b, t, HC * DV), BF16), jax.ShapeDtypeStruct((b, HC, DK, DV), F32)],
        compiler_params=_cparams(("parallel", "arbitrary")),
        name="gdn_scan",
    )(as3(u), as3(w), as3(qd), as3(kd), as3(qk), as3(egl), as3(cg), s0, gain_row)
    return out.reshape(n, HC * DV), s_new


def _route(logits):
    lane = lax.broadcasted_iota(jnp.int32, logits.shape, 1).astype(F32)

    def argmax_first(x):
        mx = jnp.max(x, axis=-1, keepdims=True)
        return mx, jnp.min(jnp.where(x == mx, lane, float(LANES)), axis=-1, keepdims=True)

    lg = jnp.where(lane < N_GROUPS, logits, NEG)
    mg, g_top = argmax_first(lg)
    pg_top = 1.0 / jnp.sum(jnp.exp(lg - mg), axis=-1, keepdims=True)
    lo = N_GROUPS + g_top * EXPERTS_PER_GROUP
    le = jnp.where((lane >= lo) & (lane < lo + EXPERTS_PER_GROUP), logits, NEG)
    m1, i1 = argmax_first(le)
    m2, i2 = argmax_first(jnp.where(lane == i1, NEG, le))
    se = jnp.sum(jnp.exp(le - m1), axis=-1, keepdims=True)
    pe1 = 1.0 / se
    pe2 = jnp.exp(m2 - m1) / se
    w1 = pg_top * pe1 / (pe1 + pe2)
    w2 = pg_top * pe2 / (pe1 + pe2)
    out = jnp.where(lane == 0, i1 - N_GROUPS, 0.0)
    out = jnp.where(lane == 1, i2 - N_GROUPS, out)
    out = jnp.where(lane == 2, w1, out)
    return jnp.where(lane == 3, w2, out)


def _merge_kernel(x_ref, a_ref, b_ref, c_ref, gm_ref, wg_ref, wb_ref, wo_ref, gf_ref, wr_ref,
                  x1_ref, h2_ref, route_ref):
    x = x_ref[...]
    d = x.shape[-1]
    h = _rms(x, gm_ref[...]).astype(BF16)
    mix = None
    for n, br in enumerate((a_ref, b_ref, c_ref)):
        term = jax.nn.sigmoid(_dot(h, wg_ref[:, n * d:(n + 1) * d])) * _dot(br[...], wb_ref[n])
        mix = term if mix is None else mix + term
    x1 = x + _dot(mix.astype(BF16), wo_ref[...])
    x1_ref[...] = x1
    h2 = _rms(x1, gf_ref[...]).astype(BF16)
    h2_ref[...] = h2
    route_ref[...] = _route(_dot(h2, wr_ref[...]))


def _merge(x2d, out_a, out_b, out_c, gm_row, w_gate, w_branch, w_out, gf_row, w_router):
    n, d = x2d.shape
    tm = _row_tile(n, ROW_TILE)
    row = lambda i: (i, 0)
    return pl.pallas_call(
        _merge_kernel,
        grid=(n // tm,),
        in_specs=[
            pl.BlockSpec((tm, d), row), pl.BlockSpec((tm, BRANCH_W), row), pl.BlockSpec((tm, BRANCH_W), row),
            pl.BlockSpec((tm, BRANCH_W), row), _resident((1, d)), _resident(w_gate.shape),
            _resident(w_branch.shape), _resident(w_out.shape), _resident((1, d)), _resident(w_router.shape),
        ],
        out_specs=[pl.BlockSpec((tm, d), row), pl.BlockSpec((tm, d), row), pl.BlockSpec((tm, LANES), row)],
        out_shape=[jax.ShapeDtypeStruct((n, d), F32), jax.ShapeDtypeStruct((n, d), BF16),
                   jax.ShapeDtypeStruct((n, LANES), F32)],
        compiler_params=_cparams(("parallel",)),
        name="merge",
    )(x2d, out_a, out_b, out_c, gm_row, w_gate, w_branch, w_out, gf_row, w_router)


def _moe_kernel(te_ref, xs_ref, rw_ref, wg_ref, wu_ref, wd_ref, y_ref):
    del te_ref
    x = xs_ref[...]
    g = _dot(x, wg_ref[...])
    a = (g * jax.nn.sigmoid(g)) * _dot(x, wu_ref[...]) * rw_ref[...]
    y_ref[...] = _dot(a.astype(BF16), wd_ref[...])


def _moe_experts(xs, row_w, tile_expert, w_g, w_u, w_d):
    m, d = xs.shape
    f = w_g.shape[-1]
    grid_spec = pltpu.PrefetchScalarGridSpec(
        num_scalar_prefetch=1,
        grid=(m // MOE_TILE,),
        in_specs=[
            pl.BlockSpec((MOE_TILE, d), lambda i, te: (i, 0)),
            pl.BlockSpec((MOE_TILE, 1), lambda i, te: (i, 0)),
            pl.BlockSpec((None, d, f), lambda i, te: (te[i], 0, 0)),
            pl.BlockSpec((None, d, f), lambda i, te: (te[i], 0, 0)),
            pl.BlockSpec((None, f, d), lambda i, te: (te[i], 0, 0)),
        ],
        out_specs=pl.BlockSpec((MOE_TILE, d), lambda i, te: (i, 0)),
    )
    return pl.pallas_call(
        _moe_kernel,
        grid_spec=grid_spec,
        out_shape=jax.ShapeDtypeStruct((m, d), F32),
        compiler_params=_cparams(("arbitrary",)),
        name="moe_experts",
    )(tile_expert, xs, row_w, w_g, w_u, w_d)


def _combine_kernel(x_ref, y1_ref, y2_ref, gain_ref, o_ref, *, final_norm):
    x = x_ref[...] + (y1_ref[...] + y2_ref[...])
    o_ref[...] = _rms(x, gain_ref[...]) if final_norm else x


def _combine(x1, y1, y2, gain_row, final_norm):
    n, d = x1.shape
    tm = _row_tile(n, ROW_TILE)
    row = lambda i: (i, 0)
    return pl.pallas_call(
        functools.partial(_combine_kernel, final_norm=final_norm),
        grid=(n // tm,),
        in_specs=[pl.BlockSpec((tm, d), row), pl.BlockSpec((tm, d), row), pl.BlockSpec((tm, d), row),
                  _resident((1, d))],
        out_specs=pl.BlockSpec((tm, d), row),
        out_shape=jax.ShapeDtypeStruct((n, d), F32),
        compiler_params=_cparams(("parallel",)),
        name="moe_combine",
    )(x1, y1, y2, gain_row)


def _dispatch_tables(route, tile):
    n = route.shape[0]
    eid = route[:, 0:2].astype(jnp.int32).reshape(-1)
    wsel = route[:, 2:4].reshape(-1)
    n_assign = 2 * n
    m_pad = -(-(n_assign + N_EXPERTS * (tile - 1)) // tile) * tile
    experts = jnp.arange(N_EXPERTS, dtype=jnp.int32)
    order = jnp.argsort(eid, stable=True).astype(jnp.int32)
    rank = jnp.argsort(order).astype(jnp.int32)
    counts = jnp.sum((eid[:, None] == experts[None, :]).astype(jnp.int32), axis=0)
    padded = (counts + tile - 1) // tile * tile
    ends_p = jnp.cumsum(padded)
    starts_p = ends_p - padded
    starts = jnp.cumsum(counts) - counts
    slot_row = (starts_p[eid] + rank - starts[eid]).reshape(n, 2)
    tile_start = jnp.arange(m_pad // tile, dtype=jnp.int32) * tile
    tile_expert = jnp.sum((ends_p[None, :] <= tile_start[:, None]).astype(jnp.int32), axis=1)
    tile_expert = jnp.minimum(tile_expert, N_EXPERTS - 1)
    r = jnp.arange(m_pad, dtype=jnp.int32)
    e_r = jnp.repeat(tile_expert, tile)
    pos = r - starts_p[e_r]
    valid = pos < counts[e_r]
    assign = order[jnp.clip(starts[e_r] + pos, 0, n_assign - 1)]
    row_src = jnp.where(valid, assign // 2, 0)
    row_w = jnp.where(valid, wsel[assign], 0.0)
    return row_src, row_w.reshape(m_pad, 1), slot_row, tile_expert


def _moe(x1, h2, route, w_g, w_u, w_d, gain_row, final_norm):
    row_src, row_w, slot_row, tile_expert = _dispatch_tables(route, MOE_TILE)
    ys = _moe_experts(h2[row_src], row_w, tile_expert, w_g, w_u, w_d)
    return _combine(x1, ys[slot_row[:, 0]], ys[slot_row[:, 1]], gain_row, final_norm)


def _lane_row(vec, offset):
    return jnp.zeros((1, LANES), F32).at[0, offset:offset + vec.shape[0]].set(vec.astype(F32))


def _layer_weights(l, w_in, norm_mixer, diff_lambda, diff_norm_gain, gdn_conv_w, gdn_a_log, gdn_dt_bias,
                   gdn_norm_gain, w_branch, w_out, norm_ffn, router_group, router_expert, expert_w_gate,
                   expert_w_up, expert_w_down):
    d = w_in.shape[1]
    w_mix = jnp.pad(w_in[l, :, :N_MIX_COLS], ((0, 0), (0, N_MIX_PAD - N_MIX_COLS))).astype(BF16)
    w_router = jnp.concatenate([router_group[l], router_expert[l]], axis=1)
    w_router = jnp.pad(w_router, ((0, 0), (0, LANES - w_router.shape[1]))).astype(BF16)
    return dict(
        gm=norm_mixer[l].reshape(1, d), w_mix=w_mix, w_gate=w_in[l, :, N_MIX_COLS:].astype(BF16),
        dl=diff_lambda[l].astype(F32), dgain=diff_norm_gain[l].reshape(1, DA).astype(F32),
        conv_w=jnp.pad(gdn_conv_w[l].astype(F32), ((0, SUBLANES - CONV_W), (0, 0))),
        alog=_lane_row(gdn_a_log[l], HC), dtb=_lane_row(gdn_dt_bias[l], HC),
        ggain=gdn_norm_gain[l].reshape(1, DV).astype(F32),
        w_branch=w_branch[l].astype(BF16), w_out=w_out[l].astype(BF16), gf=norm_ffn[l].reshape(1, d),
        w_router=w_router, w_g=expert_w_gate[l].astype(BF16), w_u=expert_w_up[l].astype(BF16),
        w_d=expert_w_down[l].astype(BF16),
    )


def _slope_rows(slopes):
    return jnp.broadcast_to(jnp.asarray(slopes, F32)[:, None, None], (len(slopes), 1, LANES))


def _prompt_layer(x2d, lw, lam_init, b, t, final_gain):
    slopes_a, slopes_b = _alibi_slopes()
    q_a, kv_a, kva16, q_b, kv_b, kvb16, means, cqkv, cg, ba = _inproj_prompt(x2d, lw["gm"], lw["w_mix"])
    out_a = _diff_attn_prompt(q_a, kva16, lw["dl"], lw["dgain"], _slope_rows(slopes_a), lam_init, b, t)
    out_b = _moba_attn_prompt(q_b, kvb16, means.reshape(b, t // MOBA_BLOCK, HB * DB), _slope_rows(slopes_b), b, t)
    prev8 = jnp.zeros((b, SUBLANES, C_CONV), F32)
    s0 = jnp.zeros((b, HC, DK, DV), F32)
    out_c, s_new = _gdn(cqkv, cg, ba, prev8, s0, lw["conv_w"], lw["alog"], lw["dtb"], lw["ggain"], b, t, t)
    x1, h2, route = _merge(x2d, out_a, out_b, out_c, lw["gm"], lw["w_gate"], lw["w_branch"], lw["w_out"],
                           lw["gf"], lw["w_router"])
    gain = lw["gf"] if final_gain is None else final_gain
    x2 = _moe(x1, h2, route, lw["w_g"], lw["w_u"], lw["w_d"], gain, final_gain is not None)
    conv_new = cqkv.reshape(b, t, C_CONV)[:, t - (CONV_W - 1):]
    return x2, kv_a, kv_b, s_new, conv_new


def _sample_layer(x2d, lw, lam_init, bs, ts, layer, cache_a, cache_b, pt_flat, n_pages, page, conv_prev, s_prev,
                  final_gain):
    slopes_a, slopes_b = _alibi_slopes()
    n = bs * ts
    q_a, kv_a, q_b, kv_b, cqkv, cg, ba = _inproj(x2d, lw["gm"], lw["w_mix"], F32)
    rows = page * 2 * HA
    out_a = _diff_attn_sample(q_a.reshape(bs, ts, -1), kv_a.reshape(bs, ts, -1),
                              cache_a.reshape(cache_a.shape[0], cache_a.shape[1], rows, DA), pt_flat,
                              lw["dl"], lw["dgain"], slopes_a, lam_init, layer, n_pages, page)
    means = _moba_means_sample(cache_b, pt_flat, layer, bs, n_pages, page)
    means = jnp.pad(means, ((0, 0), (0, 0), (0, LANES - means.shape[2]), (0, 0)))
    out_b = _moba_attn_sample(q_b.reshape(bs, ts, -1), kv_b.reshape(bs, ts, -1), means,
                              cache_b.reshape(cache_b.shape[0], cache_b.shape[1], rows, DB), pt_flat,
                              slopes_b, layer, n_pages, page)
    tpad = GDN_CHUNK

    def pad_t(a):
        return jnp.pad(a.reshape(bs, ts, -1), ((0, 0), (0, tpad - ts), (0, 0))).reshape(bs * tpad, -1)

    prev8 = jnp.pad(conv_prev.astype(F32), ((0, 0), (SUBLANES - (CONV_W - 1), 0), (0, 0)))
    out_c, s_new = _gdn(pad_t(cqkv), pad_t(cg), pad_t(ba), prev8, s_prev.astype(F32), lw["conv_w"], lw["alog"],
                        lw["dtb"], lw["ggain"], bs, tpad, ts)
    out_c = out_c.reshape(bs, tpad, -1)[:, :ts].reshape(n, -1)
    x1, h2, route = _merge(x2d, out_a.reshape(n, -1).astype(BF16), out_b.reshape(n, -1).astype(BF16), out_c,
                           lw["gm"], lw["w_gate"], lw["w_branch"], lw["w_out"], lw["gf"], lw["w_router"])
    gain = lw["gf"] if final_gain is None else final_gain
    x2 = _moe(x1, h2, route, lw["w_g"], lw["w_u"], lw["w_d"], gain, final_gain is not None)
    c_in = jnp.concatenate([conv_prev.astype(F32), cqkv.reshape(bs, ts, C_CONV)], axis=1)
    return x2, kv_a, kv_b, s_new, c_in[:, ts:]


def kernel(x_prompt, x_sample, cache_diff_kv, cache_moba_kv, state_gdn, state_conv, page_table, norm_mixer, w_in, diff_lambda, diff_norm_gain, gdn_conv_w, gdn_a_log, gdn_dt_bias, gdn_norm_gain, w_branch, w_out, norm_ffn, router_group, router_expert, expert_w_gate, expert_w_up, expert_w_down, norm_final):
    bp, tp, d = x_prompt.shape
    bs, ts, _ = x_sample.shape
    depth = w_in.shape[0]
    n_pages = page_table.shape[1]
    page = cache_diff_kv.shape[2]
    pt_flat = page_table.reshape(-1).astype(jnp.int32)
    final_gain = norm_final.reshape(1, d)

    xp = x_prompt.reshape(bp * tp, d)
    xs = x_sample.reshape(bs * ts, d)
    outs_p = [[], [], [], []]
    outs_s = [[], [], [], []]
    for l in range(depth):
        lw = _layer_weights(l, w_in, norm_mixer, diff_lambda, diff_norm_gain, gdn_conv_w, gdn_a_log, gdn_dt_bias,
                            gdn_norm_gain, w_branch, w_out, norm_ffn, router_group, router_expert, expert_w_gate,
                            expert_w_up, expert_w_down)
        lam_init = 0.8 - 0.6 * math.exp(-0.3 * l)
        fg = final_gain if l == depth - 1 else None
        xp, kva, kvb, sg, sc = _prompt_layer(xp, lw, lam_init, bp, tp, fg)
        for acc, v in zip(outs_p, (kva.reshape(bp, tp, 2, HA, DA), kvb.reshape(bp, tp, 2, HB, DB), sg, sc)):
            acc.append(v)
        xs, kva, kvb, sg, sc = _sample_layer(xs, lw, lam_init, bs, ts, l, cache_diff_kv, cache_moba_kv, pt_flat,
                                             n_pages, page, state_conv[l], state_gdn[l], fg)
        for acc, v in zip(outs_s, (kva.reshape(bs, ts, 2, HA, DA), kvb.reshape(bs, ts, 2, HB, DB),
                                   sg.astype(state_gdn.dtype), sc)):
            acc.append(v)
    return (xp.reshape(bp, tp, d), xs.reshape(bs, ts, d),
            jnp.stack(outs_p[0]), jnp.stack(outs_p[1]), jnp.stack(outs_p[2]), jnp.stack(outs_p[3]),
            jnp.stack(outs_s[0]), jnp.stack(outs_s[1]), jnp.stack(outs_s[2]), jnp.stack(outs_s[3]))
```

```python
import functools
import math

import jax
import jax.numpy as jnp
from jax import lax
from jax.experimental import pallas as pl
from jax.experimental.pallas import tpu as pltpu

HA, DA = 4, 128
DA_HALF = DA // 2
HB, DB = 4, 128
MOBA_BLOCK = 256
MOBA_TOPK = 3
HC, DK, DV = 4, 128, 128
CONV_W = 4
GDN_CHUNK = 64
C_CONV = HC * (2 * DK + DV)
N_BRANCH = 3
BRANCH_W = HA * DA
N_GROUPS = 4
EXPERTS_PER_GROUP = 8
N_EXPERTS = N_GROUPS * EXPERTS_PER_GROUP
D_EXPERT = 256
RMS_EPS = 1e-6

LANES = 128
SUBLANES = 8
VMEM_LIMIT_BYTES = 56 * 1024 * 1024

NEG = -1e30
F32 = jnp.float32
BF16 = jnp.bfloat16

_SPLITS = (HA * DA, HA * DA, HA * DA, HB * DB, HB * DB, HB * DB, C_CONV, HC * DV, HC, HC)
_OFF = [0]
for _s in _SPLITS:
    _OFF.append(_OFF[-1] + _s)
N_MIX_COLS = _OFF[-1]
N_MIX_PAD = -(-N_MIX_COLS // LANES) * LANES
BA_COL = _OFF[8]

ROW_TILE = 512
ATTN_TQ = 256
MOE_TILE = 256
PAGES_PER_STEP = 16


def _alibi_slopes():
    n = HA + HB
    s = [2.0 ** (-8.0 * i / n) for i in range(1, n + 1)]
    return s[0::2], s[1::2]


def _cparams(semantics):
    return pltpu.CompilerParams(dimension_semantics=semantics, vmem_limit_bytes=VMEM_LIMIT_BYTES)


def _row_tile(n, pref):
    return pref if n % pref == 0 else n


def _resident(shape):
    zeros = (0,) * len(shape)
    return pl.BlockSpec(shape, lambda *_: zeros, pipeline_mode=pl.Buffered(1))


def _rms(x, gain_row):
    return x * lax.rsqrt(jnp.mean(x * x, axis=-1, keepdims=True) + RMS_EPS) * gain_row


def _dot(a, b, **kw):
    return jnp.dot(a, b, preferred_element_type=F32, **kw)


def _dot_nt(a, b, **kw):
    return lax.dot_general(a, b, (((1,), (1,)), ((), ())), preferred_element_type=F32, **kw)


def _dot_tn(a, b, **kw):
    return lax.dot_general(a, b, (((0,), (0,)), ((), ())), preferred_element_type=F32, **kw)


def _inproj_kernel(x_ref, g_ref, w_ref, qa_ref, kva_ref, qb_ref, kvb_ref, cqkv_ref, cg_ref, ba_ref):
    h = _rms(x_ref[...], g_ref[...]).astype(BF16)

    def mm(lo, hi):
        return _dot(h, w_ref[:, lo:hi])

    qa_ref[...] = mm(_OFF[0], _OFF[1]).astype(qa_ref.dtype)
    kva_ref[...] = mm(_OFF[1], _OFF[3])
    qb_ref[...] = mm(_OFF[3], _OFF[4]).astype(qb_ref.dtype)
    kvb_ref[...] = mm(_OFF[4], _OFF[6])
    cqkv_ref[...] = mm(_OFF[6], _OFF[7])
    cg_ref[...] = mm(_OFF[7], _OFF[8])
    ba_ref[...] = mm(BA_COL, N_MIX_PAD)


def _inproj(x2d, gain_row, w_mix, q_dtype):
    n, d = x2d.shape
    tm = _row_tile(n, ROW_TILE)
    widths = (HA * DA, 2 * HA * DA, HB * DB, 2 * HB * DB, C_CONV, HC * DV, N_MIX_PAD - BA_COL)
    dtypes = (q_dtype, F32, q_dtype, F32, F32, F32, F32)
    return pl.pallas_call(
        _inproj_kernel,
        grid=(n // tm,),
        in_specs=[pl.BlockSpec((tm, d), lambda i: (i, 0)), _resident((1, d)), _resident(w_mix.shape)],
        out_specs=[pl.BlockSpec((tm, w), lambda i: (i, 0)) for w in widths],
        out_shape=[jax.ShapeDtypeStruct((n, w), dt) for w, dt in zip(widths, dtypes)],
        compiler_params=_cparams(("parallel",)),
        name="inproj",
    )(x2d, gain_row, w_mix)


KV_CHUNKS = 2 * HA * DA // LANES


def _store_kv_rows(out_ref, kv, tm):
    for c in range(KV_CHUNKS):
        out_ref[pl.ds(c, tm, stride=KV_CHUNKS), :] = kv[:, c * LANES:(c + 1) * LANES]


def _inproj_prompt_kernel(x_ref, g_ref, w_ref, qa_ref, kva_ref, kva16_ref, qb_ref, kvb_ref, kvb16_ref, means_ref,
                          cqkv_ref, cg_ref, ba_ref, *, tm):
    h = _rms(x_ref[...], g_ref[...]).astype(BF16)

    def mm(lo, hi):
        return _dot(h, w_ref[:, lo:hi])

    qa_ref[...] = mm(_OFF[0], _OFF[1]).astype(BF16)
    kva = mm(_OFF[1], _OFF[3])
    kva16_ref[...] = kva.astype(BF16)
    _store_kv_rows(kva_ref, kva, tm)
    qb_ref[...] = mm(_OFF[3], _OFF[4]).astype(BF16)
    kvb = mm(_OFF[4], _OFF[6])
    kvb16_ref[...] = kvb.astype(BF16)
    _store_kv_rows(kvb_ref, kvb, tm)
    for jb in range(tm // MOBA_BLOCK):
        blk = kvb[jb * MOBA_BLOCK:(jb + 1) * MOBA_BLOCK, :HB * DB]
        means_ref[jb:jb + 1, :] = jnp.sum(blk, axis=0, keepdims=True) * (1.0 / MOBA_BLOCK)
    cqkv_ref[...] = mm(_OFF[6], _OFF[7])
    cg_ref[...] = mm(_OFF[7], _OFF[8])
    ba_ref[...] = mm(BA_COL, N_MIX_PAD)


def _inproj_prompt(x2d, gain_row, w_mix):
    n, d = x2d.shape
    tm = ROW_TILE
    assert n % tm == 0 and tm % MOBA_BLOCK == 0 and HA * DA == HB * DB
    row = lambda i: (i, 0)
    kvw = 2 * HA * DA
    out = [
        (pl.BlockSpec((tm, HA * DA), row), jax.ShapeDtypeStruct((n, HA * DA), BF16)),
        (pl.BlockSpec((tm * KV_CHUNKS, LANES), row), jax.ShapeDtypeStruct((n * KV_CHUNKS, LANES), F32)),
        (pl.BlockSpec((tm, kvw), row), jax.ShapeDtypeStruct((n, kvw), BF16)),
        (pl.BlockSpec((tm, HB * DB), row), jax.ShapeDtypeStruct((n, HB * DB), BF16)),
        (pl.BlockSpec((tm * KV_CHUNKS, LANES), row), jax.ShapeDtypeStruct((n * KV_CHUNKS, LANES), F32)),
        (pl.BlockSpec((tm, kvw), row), jax.ShapeDtypeStruct((n, kvw), BF16)),
        (pl.BlockSpec((None, tm // MOBA_BLOCK, HB * DB), lambda i: (i, 0, 0)),
         jax.ShapeDtypeStruct((n // tm, tm // MOBA_BLOCK, HB * DB), F32)),
        (pl.BlockSpec((tm, C_CONV), row), jax.ShapeDtypeStruct((n, C_CONV), F32)),
        (pl.BlockSpec((tm, HC * DV), row), jax.ShapeDtypeStruct((n, HC * DV), F32)),
        (pl.BlockSpec((tm, N_MIX_PAD - BA_COL), row), jax.ShapeDtypeStruct((n, N_MIX_PAD - BA_COL), F32)),
    ]
    return pl.pallas_call(
        functools.partial(_inproj_prompt_kernel, tm=tm),
        grid=(n // tm,),
        in_specs=[pl.BlockSpec((tm, d), row), _resident((1, d)), _resident(w_mix.shape)],
        out_specs=[o[0] for o in out],
        out_shape=[o[1] for o in out],
        compiler_params=_cparams(("parallel",)),
        name="inproj_prompt",
    )(x2d, gain_row, w_mix)


def _diff_lambda(dl, lam_init):
    s1 = jnp.sum(dl[0:1, :] * dl[1:2, :], axis=1, keepdims=True)
    s2 = jnp.sum(dl[2:3, :] * dl[3:4, :], axis=1, keepdims=True)
    return jnp.exp(s1) - jnp.exp(s2) + lam_init


def _split_halves(q):
    lane = lax.broadcasted_iota(jnp.int32, q.shape, 1)
    zero = jnp.zeros_like(q)
    return jnp.concatenate([jnp.where(lane < DA_HALF, q, zero), jnp.where(lane >= DA_HALF, q, zero)], axis=0)


def _sample_softmax_pv(s_sc, v_sc, qs, kvn_ref, n_heads, tok, slopes, scale, ts, lam=None, bias=None):
    heads = range(n_heads)
    d = qs[0].shape[1]
    kn = [_bf16_round(kvn_ref[:, h * d:(h + 1) * d]) for h in heads]
    vn = [_bf16_round(kvn_ref[:, (n_heads + h) * d:(n_heads + h + 1) * d]) for h in heads]
    fresh = [[jnp.where(tok >= j, jnp.sum(qs[h] * kn[h][j:j + 1, :], axis=-1, keepdims=True) * scale
                        - slopes[h] * (tok - j).astype(F32), NEG) for j in range(ts)] for h in heads]
    s_all = [s_sc[h] if bias is None else s_sc[h] + bias[h] for h in heads]
    m = [functools.reduce(jnp.maximum, fresh[h] + [jnp.max(s_all[h], axis=-1, keepdims=True)]) for h in heads]
    p = [jnp.exp(s_all[h] - m[h]) for h in heads]
    pf = [[jnp.exp(f - m[h]) for f in fresh[h]] for h in heads]
    inv = [1.0 / functools.reduce(jnp.add, pf[h] + [jnp.sum(p[h], axis=-1, keepdims=True)]) for h in heads]
    p = [p[h] * inv[h] for h in heads]
    pf = [[x * inv[h] for x in pf[h]] for h in heads]
    if lam is not None:
        p = [x[:ts] - lam * x[ts:] for x in p]
        pf = [[x[:ts] - lam * x[ts:] for x in pf[h]] for h in heads]
        p = [jnp.concatenate([x, x], axis=0) for x in p]
    out = [_dot(p[h].astype(BF16), v_sc[h])[:ts] for h in heads]
    return [functools.reduce(jnp.add, [out[h]] + [_bf16_round(x[:ts]) * vn[h][j:j + 1, :]
                                                   for j, x in enumerate(pf[h])]) for h in heads]


def _bf16_round(x):
    return x.astype(BF16).astype(F32)


def _top_blocks(gate, eligible, axis=1):
    idx = lax.broadcasted_iota(jnp.int32, gate.shape, axis).astype(F32)
    g = jnp.where(eligible, gate, NEG)
    sel = jnp.zeros(gate.shape, F32)
    for _ in range(MOBA_TOPK):
        mx = jnp.max(g, axis=axis, keepdims=True)
        first = jnp.min(jnp.where(g == mx, idx, float(gate.shape[axis])), axis=axis, keepdims=True)
        hit = (idx == first) & (mx > 0.5 * NEG)
        sel = jnp.where(hit, 1.0, sel)
        g = jnp.where(hit, NEG, g)
    return sel


def _lane_column(mat, n):
    lane = lax.broadcasted_iota(jnp.int32, mat.shape, 1)
    return jnp.sum(jnp.where(lane == n, mat, 0.0), axis=-1, keepdims=True)


def _tile_offsets(rows, width, tq):
    row = lax.broadcasted_iota(jnp.int32, (rows, width), 0)
    col = lax.broadcasted_iota(jnp.int32, (rows, width), 1)
    return (row & (tq - 1)) - col


ATTN_ROW_GROUP = 128


def _causal_softmax_pv(qs, kbf, vbf, bias, width, tq, scale=None, block_off=None, lam=None):
    rg = min(ATTN_ROW_GROUP, tq)
    assert tq % rg == 0
    groups = [slice(i * rg, (i + 1) * rg) for i in range(tq // rg)]
    units = [(g, q) for g in groups for q in qs]
    past = width - tq
    ss = [_dot_nt(q[g], kbf[0:width, :]) for g, q in units]
    if scale is not None:
        ss = [s * scale for s in ss]
    ss = [s + bias[g, 0:width] for s, (g, _) in zip(ss, units)]
    masked = []
    for s, (g, _) in zip(ss, units):
        pieces = []
        for j in range(past // tq):
            piece = s[:, j * tq:(j + 1) * tq]
            pieces.append(piece if block_off is None else piece + block_off[g, j:j + 1])
        own = jnp.where(_tile_offsets(rg, tq, tq) + g.start >= 0, s[:, past:], NEG)
        masked.append(jnp.concatenate(pieces + [own], axis=1) if pieces else own)
    ps = [jnp.exp(s - jnp.max(s, axis=-1, keepdims=True)) for s in masked]
    ps = [p * (1.0 / jnp.sum(p, axis=-1, keepdims=True)) for p in ps]
    if len(qs) == 2:
        ps = [ps[2 * i] - lam * ps[2 * i + 1] for i in range(len(groups))]
    os = [_dot(p.astype(BF16), vbf[0:width, :]) for p in ps]
    return jnp.concatenate(os, axis=0) if len(os) > 1 else os[0]


def _diff_prompt_kernel(q_ref, kbf, vbf, dl_ref, gain_ref, slope_ref, o_ref, bias, *, tq, nq, lam_init):
    qi = pl.program_id(2)

    @pl.when(qi == 0)
    def _():
        bias[...] = -slope_ref[:, 0:1] * _tile_offsets(tq, bias.shape[1], tq).astype(F32)

    q2 = _split_halves(q_ref[...] * (DA_HALF ** -0.5))
    lam = _diff_lambda(dl_ref[...], lam_init)

    for v in range(nq):
        @pl.when(qi == v)
        def _(v=v):
            out = _causal_softmax_pv([q2[:tq], q2[tq:]], kbf, vbf, bias, (v + 1) * tq, tq, lam=lam)
            o_ref[...] = (_rms(out, gain_ref[...]) * (1.0 - lam_init)).astype(o_ref.dtype)


def _diff_attn_prompt(q, kv, dl, gain_row, slopes, lam_init, b, t):
    n = b * t
    tq = ATTN_TQ
    assert t % tq == 0 and tq & (tq - 1) == 0
    nq = t // tq
    assert nq <= 16
    kern = functools.partial(_diff_prompt_kernel, tq=tq, nq=nq, lam_init=lam_init)
    return pl.pallas_call(
        kern,
        grid=(b, HA, nq),
        in_specs=[
            pl.BlockSpec((tq, DA), lambda bi, h, qi: (bi * nq + qi, h)),
            pl.BlockSpec((t, DA), lambda bi, h, qi: (bi, h)),
            pl.BlockSpec((t, DA), lambda bi, h, qi: (bi, HA + h)),
            pl.BlockSpec((4, DA_HALF), lambda bi, h, qi: (0, 0)),
            pl.BlockSpec((1, DA), lambda bi, h, qi: (0, 0)),
            pl.BlockSpec((None, 1, LANES), lambda bi, h, qi: (h, 0, 0)),
        ],
        out_specs=pl.BlockSpec((tq, DA), lambda bi, h, qi: (bi * nq + qi, h)),
        out_shape=jax.ShapeDtypeStruct((n, HA * DA), BF16),
        scratch_shapes=[pltpu.VMEM((tq, t), F32)],
        compiler_params=_cparams(("parallel", "parallel", "arbitrary")),
        name="diff_attn_prompt",
    )(q, kv, kv, dl, gain_row, slopes)


def _moba_prompt_kernel(q_ref, kbf, vbf, means_ref, slope_ref, o_ref, means, bias, *, tq, nb):
    qi = pl.program_id(2)
    slope = slope_ref[:, 0:1]

    @pl.when(qi == 0)
    def _():
        bias[...] = -slope * _tile_offsets(tq, bias.shape[1], tq).astype(F32)
        means[...] = jnp.zeros(means.shape, F32)
        means[0:nb, :] = means_ref[...]

    q = q_ref[...]
    scale = DB ** -0.5
    nbp = -(-nb // SUBLANES) * SUBLANES
    gate_t = _dot_nt(means[...].astype(BF16), q)[0:nbp]
    blk = lax.broadcasted_iota(jnp.int32, gate_t.shape, 0)
    sel_t = _top_blocks(gate_t, blk < qi, axis=0)
    sel = jnp.concatenate([sel_t, jnp.zeros((LANES - nbp, tq), F32)], axis=0).T
    off = jnp.where(sel > 0.5, 0.0, NEG)

    for v in range(nb):
        @pl.when(qi == v)
        def _(v=v):
            o_ref[...] = _causal_softmax_pv([q], kbf, vbf, bias, (v + 1) * tq, tq, scale, off).astype(o_ref.dtype)


def _moba_attn_prompt(q, kv, means, slopes, b, t):
    n = b * t
    tq = ATTN_TQ
    assert tq == MOBA_BLOCK and t % tq == 0 and tq & (tq - 1) == 0
    nq = t // tq
    assert nq <= 16
    kern = functools.partial(_moba_prompt_kernel, tq=tq, nb=nq)
    return pl.pallas_call(
        kern,
        grid=(b, HB, nq),
        in_specs=[
            pl.BlockSpec((tq, DB), lambda bi, h, qi: (bi * nq + qi, h)),
            pl.BlockSpec((t, DB), lambda bi, h, qi: (bi, h)),
            pl.BlockSpec((t, DB), lambda bi, h, qi: (bi, HB + h)),
            pl.BlockSpec((None, nq, DB), lambda bi, h, qi: (bi, 0, h)),
            pl.BlockSpec((None, 1, LANES), lambda bi, h, qi: (h, 0, 0)),
        ],
        out_specs=pl.BlockSpec((tq, DB), lambda bi, h, qi: (bi * nq + qi, h)),
        out_shape=jax.ShapeDtypeStruct((n, HB * DB), BF16),
        scratch_shapes=[pltpu.VMEM((LANES, DB), F32), pltpu.VMEM((tq, t), F32)],
        compiler_params=_cparams(("parallel", "parallel", "arbitrary")),
        name="moba_attn_prompt",
    )(q, kv, kv, means, slopes)


def _head_rows(pages, kv, h, n_heads, page):
    parts = [pg[pl.ds(kv * n_heads + h, page, stride=2 * n_heads), :] for pg in pages]
    return jnp.concatenate(parts, axis=0).astype(BF16)


def _diff_sample_kernel(pt_ref, q_ref, kvn_ref, dl_ref, gain_ref, *rest, n_in, ts, page, past_len, lam_init,
                        slopes):
    del pt_ref
    pages = rest[:n_in]
    o_ref = rest[n_in]
    s_sc, v_sc = rest[n_in + 1:]
    i = pl.program_id(1)
    rows = 2 * ts
    nkeys = n_in * page
    row1 = lax.broadcasted_iota(jnp.int32, (rows, 1), 0)
    tok1 = jnp.where(row1 < ts, row1, row1 - ts)
    lane = lax.broadcasted_iota(jnp.int32, (rows, nkeys), 1)
    dist = ((past_len + tok1) - (i * nkeys + lane)).astype(F32)
    q_all = q_ref[...]
    q2 = [_split_halves(q_all[:, h * DA:(h + 1) * DA] * (DA_HALF ** -0.5)).astype(BF16) for h in range(HA)]

    off = pl.multiple_of(i * nkeys, nkeys)
    heads = range(HA)
    ss = [_dot_nt(q2[h], _head_rows(pages, 0, h, HA, page)) - slopes[h] * dist for h in heads]
    for h in heads:
        s_sc[h, :, pl.ds(off, nkeys)] = ss[h]
        v_sc[h, pl.ds(off, nkeys), :] = _head_rows(pages, 1, h, HA, page)

    @pl.when(i == pl.num_programs(1) - 1)
    def _():
        lam = _diff_lambda(dl_ref[...], lam_init)
        outs = _sample_softmax_pv(s_sc, v_sc, [q2[h].astype(F32) for h in heads], kvn_ref, HA, tok1, slopes, 1.0,
                                  ts, lam)
        for h in heads:
            o_ref[:, h * DA:(h + 1) * DA] = _rms(outs[h], gain_ref[...]) * (1.0 - lam_init)


def _page_specs(layer, n_pages, n_in, rows):
    def spec(j):
        return pl.BlockSpec((None, None, rows, LANES),
                            lambda b, i, pt: (layer, pt[b * n_pages + i * n_in + j], 0, 0))
    return [spec(j) for j in range(n_in)]


def _diff_attn_sample(q3, kvn3, cache4, pt_flat, dl, gain_row, slopes, lam_init, layer, n_pages, page):
    bs, ts, _ = q3.shape
    assert 2 * ts == SUBLANES
    n_in = min(PAGES_PER_STEP, n_pages)
    assert n_pages % n_in == 0
    rows = cache4.shape[2]
    kern = functools.partial(_diff_sample_kernel, n_in=n_in, ts=ts, page=page, past_len=n_pages * page,
                             lam_init=lam_init, slopes=slopes)
    grid_spec = pltpu.PrefetchScalarGridSpec(
        num_scalar_prefetch=1,
        grid=(bs, n_pages // n_in),
        in_specs=[
            pl.BlockSpec((None, ts, HA * DA), lambda b, i, pt: (b, 0, 0)),
            pl.BlockSpec((None, ts, 2 * HA * DA), lambda b, i, pt: (b, 0, 0)),
            pl.BlockSpec((4, DA_HALF), lambda b, i, pt: (0, 0)),
            pl.BlockSpec((1, DA), lambda b, i, pt: (0, 0)),
        ] + _page_specs(layer, n_pages, n_in, rows),
        out_specs=pl.BlockSpec((None, ts, HA * DA), lambda b, i, pt: (b, 0, 0)),
        scratch_shapes=[pltpu.VMEM((HA, 2 * ts, n_pages * page), F32), pltpu.VMEM((HA, n_pages * page, DA), BF16)],
    )
    return pl.pallas_call(
        kern,
        grid_spec=grid_spec,
        out_shape=jax.ShapeDtypeStruct((bs, ts, HA * DA), F32),
        compiler_params=_cparams(("parallel", "arbitrary")),
        name="diff_attn_sample",
    )(pt_flat, q3, kvn3, dl, gain_row, *([cache4] * n_in))


def _moba_sample_kernel(pt_ref, q_ref, kvn_ref, *rest, n_in, ts, page, past_len, slopes):
    del pt_ref
    pages = rest[:n_in]
    o_ref = rest[n_in]
    means_sc, s_sc, v_sc = rest[n_in + 1:]
    i = pl.program_id(1)
    rows = SUBLANES
    nkeys = n_in * page
    row1 = lax.broadcasted_iota(jnp.int32, (rows, 1), 0)
    tok1 = jnp.where(row1 < ts, row1, row1 - ts)
    lane = lax.broadcasted_iota(jnp.int32, (rows, nkeys), 1)
    dist = ((past_len + tok1) - (i * nkeys + lane)).astype(F32)
    scale = DB ** -0.5
    ppb = MOBA_BLOCK // page
    blocks_per_step = n_in // ppb
    nb_past = past_len // MOBA_BLOCK
    q_all = q_ref[...]
    q8 = [jnp.concatenate([q_all[:, h * DB:(h + 1) * DB]] * 2, axis=0).astype(BF16) for h in range(HB)]

    @pl.when(i == 0)
    def _():
        means_sc[...] = jnp.zeros(means_sc.shape, F32)

    off = pl.multiple_of(i * nkeys, nkeys)
    boff = pl.multiple_of(i * blocks_per_step, blocks_per_step)
    heads = range(HB)
    kparts = [[pg[pl.ds(h, page, stride=2 * HB), :] for pg in pages] for h in heads]
    ss = [_dot_nt(q8[h], jnp.concatenate(kparts[h], axis=0).astype(BF16)) * scale - slopes[h] * dist for h in heads]
    for h in heads:
        s_sc[h, :, pl.ds(off, nkeys)] = ss[h]
        v_sc[h, pl.ds(off, nkeys), :] = _head_rows(pages, 1, h, HB, page)
        sums = [jnp.sum(functools.reduce(jnp.add, kparts[h][jb * ppb:(jb + 1) * ppb]), axis=0, keepdims=True)
                for jb in range(blocks_per_step)]
        means_sc[h, pl.ds(boff, blocks_per_step), :] = jnp.concatenate(sums, axis=0) * (1.0 / MOBA_BLOCK)

    @pl.when(i == pl.num_programs(1) - 1)
    def _():
        bias = []
        for h in heads:
            gate = _dot_nt(q8[h], means_sc[h].astype(BF16))
            glane = lax.broadcasted_iota(jnp.int32, gate.shape, 1)
            block_off = jnp.where(_top_blocks(gate, glane < nb_past) > 0.5, 0.0, NEG)
            bias.append(jnp.concatenate([jnp.broadcast_to(block_off[:, j:j + 1], (rows, MOBA_BLOCK))
                                         for j in range(nb_past)], axis=1))
        outs = _sample_softmax_pv(s_sc, v_sc, [q8[h].astype(F32) for h in heads], kvn_ref, HB, tok1, slopes, scale,
                                  ts, bias=bias)
        for h in heads:
            o_ref[:, h * DB:(h + 1) * DB] = outs[h]


def _moba_attn_sample(q3, kvn3, cache4, pt_flat, slopes, layer, n_pages, page):
    bs, ts, _ = q3.shape
    assert 2 * ts == SUBLANES
    past_len = n_pages * page
    assert past_len % MOBA_BLOCK == 0 and ts <= MOBA_BLOCK and past_len // MOBA_BLOCK <= LANES
    n_in = min(PAGES_PER_STEP, n_pages)
    assert n_pages % n_in == 0 and MOBA_BLOCK % page == 0 and (n_in * page) % MOBA_BLOCK == 0
    assert n_in == n_pages or (n_in * page // MOBA_BLOCK) % SUBLANES == 0
    rows = cache4.shape[2]
    kern = functools.partial(_moba_sample_kernel, n_in=n_in, ts=ts, page=page, past_len=past_len, slopes=slopes)
    grid_spec = pltpu.PrefetchScalarGridSpec(
        num_scalar_prefetch=1,
        grid=(bs, n_pages // n_in),
        in_specs=[
            pl.BlockSpec((None, ts, HB * DB), lambda b, i, pt: (b, 0, 0)),
            pl.BlockSpec((None, ts, 2 * HB * DB), lambda b, i, pt: (b, 0, 0)),
        ] + _page_specs(layer, n_pages, n_in, rows),
        out_specs=pl.BlockSpec((None, ts, HB * DB), lambda b, i, pt: (b, 0, 0)),
        scratch_shapes=[pltpu.VMEM((HB, LANES, DB), F32), pltpu.VMEM((HB, SUBLANES, past_len), F32),
                        pltpu.VMEM((HB, past_len, DB), BF16)],
    )
    return pl.pallas_call(
        kern,
        grid_spec=grid_spec,
        out_shape=jax.ShapeDtypeStruct((bs, ts, HB * DB), F32),
        compiler_params=_cparams(("parallel", "arbitrary")),
        name="moba_attn_sample",
    )(pt_flat, q3, kvn3, *([cache4] * n_in))


def _split_bf16(x, parts):
    out = []
    for _ in range(parts):
        hi = x.astype(BF16)
        out.append(hi)
        x = x - hi.astype(F32)
    return out


def _dot3(a, b):
    a_hi, a_lo = _split_bf16(a, 2)
    b_hi, b_lo = _split_bf16(b, 2)
    return _dot(a_hi, b_hi) + (_dot(a_hi, b_lo) + _dot(a_lo, b_hi))


def _unit_lower_inverses(nmats):
    c = nmats[0].shape[0]
    row = lax.broadcasted_iota(jnp.int32, (c, c), 0)
    col = lax.broadcasted_iota(jnp.int32, (c, c), 1)
    eye = (row == col).astype(F32)
    shift = 3
    same = lax.shift_right_logical(row, shift) == lax.shift_right_logical(col, shift)
    n0 = [jnp.where(same, n, 0.0) for n in nmats]
    n2 = [_dot3(a, a) for a in n0]
    n4 = [_dot3(a, a) for a in n2]
    inv = [_dot3(eye - a, eye + b) for a, b in zip(n0, n2)]
    inv = [_dot3(a, eye + b) for a, b in zip(inv, n4)]
    while (1 << shift) < c:
        shift += 1
        same2 = lax.shift_right_logical(row, shift) == lax.shift_right_logical(col, shift)
        grow = same2 & jnp.logical_not(same)
        left = [_dot3(a, jnp.where(grow, n, 0.0)) for a, n in zip(inv, nmats)]
        inv = [a - _dot3(b, a) for a, b in zip(inv, left)]
        same = same2
    return inv


def _gdn_prep_kernel(cqkv_ref, halo_ref, ba_ref, prev_ref, cw_ref, alog_ref, dtb_ref,
                     u_ref, w_ref, qd_ref, kd_ref, qk_ref, egl_ref, xbuf, *, chunk, n_chunks, t_valid):
    c = pl.program_id(1)
    rows = n_chunks * chunk
    halo = SUBLANES
    xbuf[0:halo, :] = jnp.where(c == 0, prev_ref[...], halo_ref[...])
    xbuf[halo:halo + rows, :] = cqkv_ref[...]
    conv = cw_ref[0:1, :] * xbuf[halo - 3:halo - 3 + rows, :]
    for j in range(1, CONV_W):
        conv = conv + cw_ref[j:j + 1, :] * xbuf[halo - 3 + j:halo - 3 + j + rows, :]
    conv = conv * jax.nn.sigmoid(conv)

    rowc = lax.broadcasted_iota(jnp.int32, (rows, 1), 0)
    valid = (c * rows + rowc) < t_valid
    ba = ba_ref[...]
    beta_all = jnp.where(valid, jax.nn.sigmoid(ba), 0.0)
    g_all = jnp.where(valid, -jnp.exp(alog_ref[...]) * jax.nn.softplus(ba + dtb_ref[...]), 0.0)
    ri = lax.broadcasted_iota(jnp.int32, (chunk, chunk), 0)
    ci = lax.broadcasted_iota(jnp.int32, (chunk, chunk), 1)
    tril16 = (ri >= ci).astype(BF16)
    eye16 = (ri == ci).astype(BF16)

    probs = []
    for g in range(n_chunks):
        r0 = g * chunk
        gc_all = functools.reduce(jnp.add, [_dot(tril16, p) for p in _split_bf16(g_all[r0:r0 + chunk], 3)])
        gc_t = functools.reduce(jnp.add, [_dot_tn(p, eye16) for p in _split_bf16(gc_all, 3)])
        egl_ref[r0:r0 + chunk, :] = jnp.broadcast_to(jnp.exp(gc_all[chunk - 1:chunk, :]), (chunk, LANES))
        for h in range(HC):
            qh = conv[r0:r0 + chunk, h * DK:(h + 1) * DK]
            kh = conv[r0:r0 + chunk, (HC + h) * DK:(HC + h + 1) * DK]
            vh = conv[r0:r0 + chunk, 2 * HC * DK + h * DV:2 * HC * DK + (h + 1) * DV]
            qh = qh * lax.rsqrt(jnp.sum(qh * qh, axis=-1, keepdims=True) + 1e-6) * (DK ** -0.5)
            kh = kh * lax.rsqrt(jnp.sum(kh * kh, axis=-1, keepdims=True) + 1e-6)
            beta = _lane_column(beta_all[r0:r0 + chunk], h)
            gc = _lane_column(gc_all, HC + h)
            gc_row = gc_t[HC + h:HC + h + 1, :]
            probs.append(dict(r0=r0, h=h, qh=qh, kh=kh, vh=vh, beta=beta, gc=gc, diff=gc - gc_row,
                              gc_last=gc_row[:, chunk - 1:chunk], kbeta=kh * beta, k16=kh.astype(BF16)))
    nmats = [_dot_nt(p["kbeta"].astype(BF16), p["k16"]) * jnp.exp(jnp.where(ri > ci, p["diff"], NEG)) for p in probs]
    qks = [_dot_nt(p["qh"].astype(BF16), p["k16"]) * jnp.exp(jnp.where(ri >= ci, p["diff"], NEG)) for p in probs]
    invs = _unit_lower_inverses(nmats)
    uws = []
    for p, inv in zip(probs, invs):
        egc = jnp.exp(p["gc"])
        rhs = jnp.concatenate([(p["vh"] * p["beta"]).astype(BF16), (p["kbeta"] * egc).astype(BF16)], axis=1)
        uws.append(_dot(inv.astype(BF16), rhs))
        p["egc"] = egc
    for p, uw, qk in zip(probs, uws, qks):
        rs = slice(p["r0"], p["r0"] + chunk)
        hs = slice(p["h"] * DK, (p["h"] + 1) * DK)
        u_ref[rs, hs] = uw[:, :DV]
        w_ref[rs, hs] = uw[:, DV:].astype(BF16)
        qk_ref[rs, p["h"] * chunk:(p["h"] + 1) * chunk] = qk.astype(BF16)
        qd_ref[rs, hs] = (p["qh"] * p["egc"]).astype(BF16)
        kd_ref[rs, hs] = (p["kh"] * jnp.exp(p["gc_last"] - p["gc"])).astype(BF16)


def _gdn_scan_kernel(u_ref, w_ref, qd_ref, kd_ref, qk_ref, egl_ref, cg_ref, s0_ref, gain_ref, o_ref, s_ref,
                     *, n_seq, chunk):
    c = pl.program_id(1)

    @pl.when(c == 0)
    def _():
        s_ref[...] = s0_ref[...]

    chains = [(bb, h, slice(h * DK, (h + 1) * DK)) for bb in range(n_seq) for h in range(HC)]
    s_old = [s_ref[bb, h] for bb, h, _ in chains]
    s16 = [s.astype(BF16) for s in s_old]
    ws = [_dot(jnp.concatenate([w_ref[bb, :, hs], qd_ref[bb, :, hs]], axis=0), s)
          for (bb, _, hs), s in zip(chains, s16)]
    v16 = [(u_ref[bb, :, hs] - x[:chunk]).astype(BF16) for (bb, _, hs), x in zip(chains, ws)]
    o = [x[chunk:] + _dot(qk_ref[bb, :, h * chunk:(h + 1) * chunk], v) for (bb, h, _), x, v in zip(chains, ws, v16)]
    kv = [_dot_tn(kd_ref[bb, :, hs], v) for (bb, _, hs), v in zip(chains, v16)]
    for (bb, h, hs), s, upd, out in zip(chains, s_old, kv, o):
        decay = _lane_column(egl_ref[bb, 0:1, :], HC + h)
        s_ref[bb, h] = s * decay + upd
        gate = cg_ref[bb, :, hs]
        o_ref[bb, :, hs] = (_rms(out, gain_ref[...]) * (gate * jax.nn.sigmoid(gate))).astype(o_ref.dtype)


def _gdn(cqkv, cg, ba, prev8, s0, conv_w8, alog_row, dtb_row, gain_row, b, t, t_valid):
    chunk = GDN_CHUNK
    assert t % chunk == 0 and DK == DV
    nc = t // chunk
    n = b * t
    g = 2 if nc % 2 == 0 else 1
    rows = g * chunk
    steps = nc // g
    hb = SUBLANES
    prep = functools.partial(_gdn_prep_kernel, chunk=chunk, n_chunks=g, t_valid=t_valid)
    blk = lambda width: pl.BlockSpec((rows, width), lambda bi, c: (bi * steps + c, 0))
    u, w, qd, kd, qk, egl = pl.pallas_call(
        prep,
        grid=(b, steps),
        in_specs=[
            blk(C_CONV),
            pl.BlockSpec((hb, C_CONV), lambda bi, c: (jnp.maximum((bi * t + c * rows) // hb - 1, 0), 0)),
            blk(LANES),
            pl.BlockSpec((None, hb, C_CONV), lambda bi, c: (bi, 0, 0)),
            pl.BlockSpec((SUBLANES, C_CONV), lambda bi, c: (0, 0)),
            pl.BlockSpec((1, LANES), lambda bi, c: (0, 0)),
            pl.BlockSpec((1, LANES), lambda bi, c: (0, 0)),
        ],
        out_specs=[blk(HC * DV), blk(HC * DK), blk(HC * DK), blk(HC * DK), blk(HC * chunk), blk(LANES)],
        out_shape=[jax.ShapeDtypeStruct((n, HC * DV), F32), jax.ShapeDtypeStruct((n, HC * DK), BF16),
                   jax.ShapeDtypeStruct((n, HC * DK), BF16), jax.ShapeDtypeStruct((n, HC * DK), BF16),
                   jax.ShapeDtypeStruct((n, HC * chunk), BF16), jax.ShapeDtypeStruct((n, LANES), F32)],
        scratch_shapes=[pltpu.VMEM((rows + hb, C_CONV), F32)],
        compiler_params=_cparams(("parallel", "parallel")),
        name="gdn_prep",
    )(cqkv, cqkv, ba, prev8, conv_w8, alog_row, dtb_row)

    n_seq = 4 if b % 4 == 0 else (2 if b % 2 == 0 else 1)
    scan = functools.partial(_gdn_scan_kernel, n_seq=n_seq, chunk=chunk)
    seq = lambda width: pl.BlockSpec((n_seq, chunk, width), lambda bg, c: (bg, c, 0))
    state = pl.BlockSpec((n_seq, HC, DK, DV), lambda bg, c: (bg, 0, 0, 0))
    as3 = lambda a: a.reshape(b, t, a.shape[-1])
    out, s_new = pl.pallas_call(
        scan,
        grid=(b // n_seq, nc),
        in_specs=[seq(HC * DV), seq(HC * DK), seq(HC * DK), seq(HC * DK), seq(HC * chunk), seq(LANES),
                  seq(HC * DV), state, pl.BlockSpec((1, DV), lambda bg, c: (0, 0))],
        out_specs=[seq(HC * DV), state],
        out_shape=[jax.ShapeDtypeStruct((b, t, HC * DV), BF16), jax.ShapeDtypeStruct((b, HC, DK, DV), F32)],
        compiler_params=_cparams(("parallel", "arbitrary")),
        name="gdn_scan",
    )(as3(u), as3(w), as3(qd), as3(kd), as3(qk), as3(egl), as3(cg), s0, gain_row)
    return out.reshape(n, HC * DV), s_new


def _route(logits):
    lane = lax.broadcasted_iota(jnp.int32, logits.shape, 1).astype(F32)

    def argmax_first(x):
        mx = jnp.max(x, axis=-1, keepdims=True)
        return mx, jnp.min(jnp.where(x == mx, lane, float(LANES)), axis=-1, keepdims=True)

    lg = jnp.where(lane < N_GROUPS, logits, NEG)
    mg, g_top = argmax_first(lg)
    pg_top = 1.0 / jnp.sum(jnp.exp(lg - mg), axis=-1, keepdims=True)
    lo = N_GROUPS + g_top * EXPERTS_PER_GROUP
    le = jnp.where((lane >= lo) & (lane < lo + EXPERTS_PER_GROUP), logits, NEG)
    m1, i1 = argmax_first(le)
    m2, i2 = argmax_first(jnp.where(lane == i1, NEG, le))
    se = jnp.sum(jnp.exp(le - m1), axis=-1, keepdims=True)
    pe1 = 1.0 / se
    pe2 = jnp.exp(m2 - m1) / se
    w1 = pg_top * pe1 / (pe1 + pe2)
    w2 = pg_top * pe2 / (pe1 + pe2)
    out = jnp.where(lane == 0, i1 - N_GROUPS, 0.0)
    out = jnp.where(lane == 1, i2 - N_GROUPS, out)
    out = jnp.where(lane == 2, w1, out)
    return jnp.where(lane == 3, w2, out)


def _merge_kernel(x_ref, a_ref, b_ref, c_ref, gm_ref, wg_ref, wb_ref, wo_ref, gf_ref, wr_ref,
                  x1_ref, h2_ref, route_ref):
    x = x_ref[...]
    d = x.shape[-1]
    h = _rms(x, gm_ref[...]).astype(BF16)
    mix = None
    for n, br in enumerate((a_ref, b_ref, c_ref)):
        term = jax.nn.sigmoid(_dot(h, wg_ref[:, n * d:(n + 1) * d])) * _dot(br[...], wb_ref[n])
        mix = term if mix is None else mix + term
    x1 = x + _dot(mix.astype(BF16), wo_ref[...])
    x1_ref[...] = x1
    h2 = _rms(x1, gf_ref[...]).astype(BF16)
    h2_ref[...] = h2
    route_ref[...] = _route(_dot(h2, wr_ref[...]))


def _merge(x2d, out_a, out_b, out_c, gm_row, w_gate, w_branch, w_out, gf_row, w_router):
    n, d = x2d.shape
    tm = _row_tile(n, ROW_TILE)
    row = lambda i: (i, 0)
    return pl.pallas_call(
        _merge_kernel,
        grid=(n // tm,),
        in_specs=[
            pl.BlockSpec((tm, d), row), pl.BlockSpec((tm, BRANCH_W), row), pl.BlockSpec((tm, BRANCH_W), row),
            pl.BlockSpec((tm, BRANCH_W), row), _resident((1, d)), _resident(w_gate.shape),
            _resident(w_branch.shape), _resident(w_out.shape), _resident((1, d)), _resident(w_router.shape),
        ],
        out_specs=[pl.BlockSpec((tm, d), row), pl.BlockSpec((tm, d), row), pl.BlockSpec((tm, LANES), row)],
        out_shape=[jax.ShapeDtypeStruct((n, d), F32), jax.ShapeDtypeStruct((n, d), BF16),
                   jax.ShapeDtypeStruct((n, LANES), F32)],
        compiler_params=_cparams(("parallel",)),
        name="merge",
    )(x2d, out_a, out_b, out_c, gm_row, w_gate, w_branch, w_out, gf_row, w_router)


def _moe_kernel(te_ref, xs_ref, rw_ref, wg_ref, wu_ref, wd_ref, y_ref):
    del te_ref
    x = xs_ref[...]
    g = _dot(x, wg_ref[...])
    a = (g * jax.nn.sigmoid(g)) * _dot(x, wu_ref[...]) * rw_ref[...]
    y_ref[...] = _dot(a.astype(BF16), wd_ref[...])


def _moe_experts(xs, row_w, tile_expert, w_g, w_u, w_d):
    m, d = xs.shape
    f = w_g.shape[-1]
    grid_spec = pltpu.PrefetchScalarGridSpec(
        num_scalar_prefetch=1,
        grid=(m // MOE_TILE,),
        in_specs=[
            pl.BlockSpec((MOE_TILE, d), lambda i, te: (i, 0)),
            pl.BlockSpec((MOE_TILE, 1), lambda i, te: (i, 0)),
            pl.BlockSpec((None, d, f), lambda i, te: (te[i], 0, 0)),
            pl.BlockSpec((None, d, f), lambda i, te: (te[i], 0, 0)),
            pl.BlockSpec((None, f, d), lambda i, te: (te[i], 0, 0)),
        ],
        out_specs=pl.BlockSpec((MOE_TILE, d), lambda i, te: (i, 0)),
    )
    return pl.pallas_call(
        _moe_kernel,
        grid_spec=grid_spec,
        out_shape=jax.ShapeDtypeStruct((m, d), F32),
        compiler_params=_cparams(("arbitrary",)),
        name="moe_experts",
    )(tile_expert, xs, row_w, w_g, w_u, w_d)


def _combine_kernel(x_ref, y1_ref, y2_ref, gain_ref, o_ref, *, final_norm):
    x = x_ref[...] + (y1_ref[...] + y2_ref[...])
    o_ref[...] = _rms(x, gain_ref[...]) if final_norm else x


def _combine(x1, y1, y2, gain_row, final_norm):
    n, d = x1.shape
    tm = _row_tile(n, ROW_TILE)
    row = lambda i: (i, 0)
    return pl.pallas_call(
        functools.partial(_combine_kernel, final_norm=final_norm),
        grid=(n // tm,),
        in_specs=[pl.BlockSpec((tm, d), row), pl.BlockSpec((tm, d), row), pl.BlockSpec((tm, d), row),
                  _resident((1, d))],
        out_specs=pl.BlockSpec((tm, d), row),
        out_shape=jax.ShapeDtypeStruct((n, d), F32),
        compiler_params=_cparams(("parallel",)),
        name="moe_combine",
    )(x1, y1, y2, gain_row)


def _dispatch_tables(route, tile):
    n = route.shape[0]
    eid = route[:, 0:2].astype(jnp.int32).reshape(-1)
    wsel = route[:, 2:4].reshape(-1)
    n_assign = 2 * n
    m_pad = -(-(n_assign + N_EXPERTS * (tile - 1)) // tile) * tile
    experts = jnp.arange(N_EXPERTS, dtype=jnp.int32)
    order = jnp.argsort(eid, stable=True).astype(jnp.int32)
    rank = jnp.argsort(order).astype(jnp.int32)
    counts = jnp.sum((eid[:, None] == experts[None, :]).astype(jnp.int32), axis=0)
    padded = (counts + tile - 1) // tile * tile
    ends_p = jnp.cumsum(padded)
    starts_p = ends_p - padded
    starts = jnp.cumsum(counts) - counts
    slot_row = (starts_p[eid] + rank - starts[eid]).reshape(n, 2)
    tile_start = jnp.arange(m_pad // tile, dtype=jnp.int32) * tile
    tile_expert = jnp.sum((ends_p[None, :] <= tile_start[:, None]).astype(jnp.int32), axis=1)
    tile_expert = jnp.minimum(tile_expert, N_EXPERTS - 1)
    r = jnp.arange(m_pad, dtype=jnp.int32)
    e_r = jnp.repeat(tile_expert, tile)
    pos = r - starts_p[e_r]
    valid = pos < counts[e_r]
    assign = order[jnp.clip(starts[e_r] + pos, 0, n_assign - 1)]
    row_src = jnp.where(valid, assign // 2, 0)
    row_w = jnp.where(valid, wsel[assign], 0.0)
    return row_src, row_w.reshape(m_pad, 1), slot_row, tile_expert


def _moe_dispatch(h2, route):
    row_src, row_w, slot_row, tile_expert = _dispatch_tables(route, MOE_TILE)
    return h2[row_src], row_w, slot_row, tile_expert


def _moe_finish(x1, dispatched, w_g, w_u, w_d, gain_row, final_norm):
    xs, row_w, slot_row, tile_expert = dispatched
    ys = _moe_experts(xs, row_w, tile_expert, w_g, w_u, w_d)
    return _combine(x1, ys[slot_row[:, 0]], ys[slot_row[:, 1]], gain_row, final_norm)


def _lane_row(vec, offset):
    return jnp.zeros((1, LANES), F32).at[0, offset:offset + vec.shape[0]].set(vec.astype(F32))


def _layer_weights(l, w_in, norm_mixer, diff_lambda, diff_norm_gain, gdn_conv_w, gdn_a_log, gdn_dt_bias,
                   gdn_norm_gain, w_branch, w_out, norm_ffn, router_group, router_expert, expert_w_gate,
                   expert_w_up, expert_w_down):
    d = w_in.shape[1]
    w_mix = jnp.pad(w_in[l, :, :N_MIX_COLS], ((0, 0), (0, N_MIX_PAD - N_MIX_COLS))).astype(BF16)
    w_router = jnp.concatenate([router_group[l], router_expert[l]], axis=1)
    w_router = jnp.pad(w_router, ((0, 0), (0, LANES - w_router.shape[1]))).astype(BF16)
    return dict(
        gm=norm_mixer[l].reshape(1, d), w_mix=w_mix, w_gate=w_in[l, :, N_MIX_COLS:].astype(BF16),
        dl=diff_lambda[l].astype(F32), dgain=diff_norm_gain[l].reshape(1, DA).astype(F32),
        conv_w=jnp.pad(gdn_conv_w[l].astype(F32), ((0, SUBLANES - CONV_W), (0, 0))),
        alog=_lane_row(gdn_a_log[l], HC), dtb=_lane_row(gdn_dt_bias[l], HC),
        ggain=gdn_norm_gain[l].reshape(1, DV).astype(F32),
        w_branch=w_branch[l].astype(BF16), w_out=w_out[l].astype(BF16), gf=norm_ffn[l].reshape(1, d),
        w_router=w_router, w_g=expert_w_gate[l].astype(BF16), w_u=expert_w_up[l].astype(BF16),
        w_d=expert_w_down[l].astype(BF16),
    )


def _slope_rows(slopes):
    return jnp.broadcast_to(jnp.asarray(slopes, F32)[:, None, None], (len(slopes), 1, LANES))


def _prompt_layer(x2d, lw, lam_init, b, t):
    slopes_a, slopes_b = _alibi_slopes()
    q_a, kv_a, kva16, q_b, kv_b, kvb16, means, cqkv, cg, ba = _inproj_prompt(x2d, lw["gm"], lw["w_mix"])
    out_a = _diff_attn_prompt(q_a, kva16, lw["dl"], lw["dgain"], _slope_rows(slopes_a), lam_init, b, t)
    out_b = _moba_attn_prompt(q_b, kvb16, means.reshape(b, t // MOBA_BLOCK, HB * DB), _slope_rows(slopes_b), b, t)
    prev8 = jnp.zeros((b, SUBLANES, C_CONV), F32)
    s0 = jnp.zeros((b, HC, DK, DV), F32)
    out_c, s_new = _gdn(cqkv, cg, ba, prev8, s0, lw["conv_w"], lw["alog"], lw["dtb"], lw["ggain"], b, t, t)
    x1, h2, route = _merge(x2d, out_a, out_b, out_c, lw["gm"], lw["w_gate"], lw["w_branch"], lw["w_out"],
                           lw["gf"], lw["w_router"])
    conv_new = cqkv.reshape(b, t, C_CONV)[:, t - (CONV_W - 1):]
    return (x1, _moe_dispatch(h2, route)), kv_a, kv_b, s_new, conv_new


def _layer_finish(pending, lw, final_gain):
    x1, dispatched = pending
    gain = lw["gf"] if final_gain is None else final_gain
    return _moe_finish(x1, dispatched, lw["w_g"], lw["w_u"], lw["w_d"], gain, final_gain is not None)


def _sample_layer(x2d, lw, lam_init, bs, ts, layer, cache_a, cache_b, pt_flat, n_pages, page, conv_prev, s_prev):
    slopes_a, slopes_b = _alibi_slopes()
    n = bs * ts
    q_a, kv_a, q_b, kv_b, cqkv, cg, ba = _inproj(x2d, lw["gm"], lw["w_mix"], F32)
    rows = page * 2 * HA
    out_a = _diff_attn_sample(q_a.reshape(bs, ts, -1), kv_a.reshape(bs, ts, -1),
                              cache_a.reshape(cache_a.shape[0], cache_a.shape[1], rows, DA), pt_flat,
                              lw["dl"], lw["dgain"], slopes_a, lam_init, layer, n_pages, page)
    out_b = _moba_attn_sample(q_b.reshape(bs, ts, -1), kv_b.reshape(bs, ts, -1),
                              cache_b.reshape(cache_b.shape[0], cache_b.shape[1], rows, DB), pt_flat,
                              slopes_b, layer, n_pages, page)
    tpad = GDN_CHUNK

    def pad_t(a):
        return jnp.pad(a.reshape(bs, ts, -1), ((0, 0), (0, tpad - ts), (0, 0))).reshape(bs * tpad, -1)

    prev8 = jnp.pad(conv_prev.astype(F32), ((0, 0), (SUBLANES - (CONV_W - 1), 0), (0, 0)))
    out_c, s_new = _gdn(pad_t(cqkv), pad_t(cg), pad_t(ba), prev8, s_prev.astype(F32), lw["conv_w"], lw["alog"],
                        lw["dtb"], lw["ggain"], bs, tpad, ts)
    out_c = out_c.reshape(bs, tpad, -1)[:, :ts].reshape(n, -1)
    x1, h2, route = _merge(x2d, out_a.reshape(n, -1).astype(BF16), out_b.reshape(n, -1).astype(BF16), out_c,
                           lw["gm"], lw["w_gate"], lw["w_branch"], lw["w_out"], lw["gf"], lw["w_router"])
    c_in = jnp.concatenate([conv_prev.astype(F32), cqkv.reshape(bs, ts, C_CONV)], axis=1)
    return (x1, _moe_dispatch(h2, route)), kv_a, kv_b, s_new, c_in[:, ts:]


def kernel(x_prompt, x_sample, cache_diff_kv, cache_moba_kv, state_gdn, state_conv, page_table, norm_mixer, w_in, diff_lambda, diff_norm_gain, gdn_conv_w, gdn_a_log, gdn_dt_bias, gdn_norm_gain, w_branch, w_out, norm_ffn, router_group, router_expert, expert_w_gate, expert_w_up, expert_w_down, norm_final):
    bp, tp, d = x_prompt.shape
    bs, ts, _ = x_sample.shape
    depth = w_in.shape[0]
    n_pages = page_table.shape[1]
    page = cache_diff_kv.shape[2]
    pt_flat = page_table.reshape(-1).astype(jnp.int32)
    final_gain = norm_final.reshape(1, d)

    xp = x_prompt.reshape(bp * tp, d)
    xs = x_sample.reshape(bs * ts, d)
    outs_p = [[], [], [], []]
    outs_s = [[], [], [], []]
    for l in range(depth):
        lw = _layer_weights(l, w_in, norm_mixer, diff_lambda, diff_norm_gain, gdn_conv_w, gdn_a_log, gdn_dt_bias,
                            gdn_norm_gain, w_branch, w_out, norm_ffn, router_group, router_expert, expert_w_gate,
                            expert_w_up, expert_w_down)
        lam_init = 0.8 - 0.6 * math.exp(-0.3 * l)
        fg = final_gain if l == depth - 1 else None
        pend_p, kva, kvb, sg, sc = _prompt_layer(xp, lw, lam_init, bp, tp)
        for acc, v in zip(outs_p, (kva.reshape(bp, tp, 2, HA, DA), kvb.reshape(bp, tp, 2, HB, DB), sg, sc)):
            acc.append(v)
        pend_s, kva, kvb, sg, sc = _sample_layer(xs, lw, lam_init, bs, ts, l, cache_diff_kv, cache_moba_kv, pt_flat,
                                                 n_pages, page, state_conv[l], state_gdn[l])
        for acc, v in zip(outs_s, (kva.reshape(bs, ts, 2, HA, DA), kvb.reshape(bs, ts, 2, HB, DB),
                                   sg.astype(state_gdn.dtype), sc)):
            acc.append(v)
        xp = _layer_finish(pend_p, lw, fg)
        xs = _layer_finish(pend_s, lw, fg)
    return (xp.reshape(bp, tp, d), xs.reshape(bs, ts, d),
            jnp.stack(outs_p[0]), jnp.stack(outs_p[1]), jnp.stack(outs_p[2]), jnp.stack(outs_p[3]),
            jnp.stack(outs_s[0]), jnp.stack(outs_s[1]), jnp.stack(outs_s[2]), jnp.stack(outs_s[3]))
```

```python
import functools
import math

import jax
import jax.numpy as jnp
from jax import lax
from jax.experimental import pallas as pl
from jax.experimental.pallas import tpu as pltpu

HA, DA = 4, 128
DA_HALF = DA // 2
HB, DB = 4, 128
MOBA_BLOCK = 256
MOBA_TOPK = 3
HC, DK, DV = 4, 128, 128
CONV_W = 4
GDN_CHUNK = 64
C_CONV = HC * (2 * DK + DV)
N_BRANCH = 3
BRANCH_W = HA * DA
N_GROUPS = 4
EXPERTS_PER_GROUP = 8
N_EXPERTS = N_GROUPS * EXPERTS_PER_GROUP
D_EXPERT = 256
RMS_EPS = 1e-6

LANES = 128
SUBLANES = 8
VMEM_LIMIT_BYTES = 56 * 1024 * 1024

NEG = -1e30
F32 = jnp.float32
BF16 = jnp.bfloat16

_SPLITS = (HA * DA, HA * DA, HA * DA, HB * DB, HB * DB, HB * DB, C_CONV, HC * DV, HC, HC)
_OFF = [0]
for _s in _SPLITS:
    _OFF.append(_OFF[-1] + _s)
N_MIX_COLS = _OFF[-1]
N_MIX_PAD = -(-N_MIX_COLS // LANES) * LANES
BA_COL = _OFF[8]

ROW_TILE = 512
ATTN_TQ = 256
MOE_TILE = 512
PAGES_PER_STEP = 16


def _alibi_slopes():
    n = HA + HB
    s = [2.0 ** (-8.0 * i / n) for i in range(1, n + 1)]
    return s[0::2], s[1::2]


def _cparams(semantics):
    return pltpu.CompilerParams(dimension_semantics=semantics, vmem_limit_bytes=VMEM_LIMIT_BYTES)


def _row_tile(n, pref):
    return pref if n % pref == 0 else n


def _resident(shape):
    zeros = (0,) * len(shape)
    return pl.BlockSpec(shape, lambda *_: zeros, pipeline_mode=pl.Buffered(1))


def _rms(x, gain_row):
    return x * lax.rsqrt(jnp.mean(x * x, axis=-1, keepdims=True) + RMS_EPS) * gain_row


def _dot(a, b, **kw):
    return jnp.dot(a, b, preferred_element_type=F32, **kw)


def _dot_nt(a, b, **kw):
    return lax.dot_general(a, b, (((1,), (1,)), ((), ())), preferred_element_type=F32, **kw)


def _dot_tn(a, b, **kw):
    return lax.dot_general(a, b, (((0,), (0,)), ((), ())), preferred_element_type=F32, **kw)


def _inproj_kernel(x_ref, g_ref, w_ref, qa_ref, kva_ref, qb_ref, kvb_ref, cqkv_ref, cg_ref, ba_ref):
    h = _rms(x_ref[...], g_ref[...]).astype(BF16)

    def mm(lo, hi):
        return _dot(h, w_ref[:, lo:hi])

    qa_ref[...] = mm(_OFF[0], _OFF[1]).astype(qa_ref.dtype)
    kva_ref[...] = mm(_OFF[1], _OFF[3])
    qb_ref[...] = mm(_OFF[3], _OFF[4]).astype(qb_ref.dtype)
    kvb_ref[...] = mm(_OFF[4], _OFF[6])
    cqkv_ref[...] = mm(_OFF[6], _OFF[7])
    cg_ref[...] = mm(_OFF[7], _OFF[8])
    ba_ref[...] = mm(BA_COL, N_MIX_PAD)


def _inproj(x2d, gain_row, w_mix, q_dtype):
    n, d = x2d.shape
    tm = _row_tile(n, ROW_TILE)
    widths = (HA * DA, 2 * HA * DA, HB * DB, 2 * HB * DB, C_CONV, HC * DV, N_MIX_PAD - BA_COL)
    dtypes = (q_dtype, F32, q_dtype, F32, F32, F32, F32)
    return pl.pallas_call(
        _inproj_kernel,
        grid=(n // tm,),
        in_specs=[pl.BlockSpec((tm, d), lambda i: (i, 0)), _resident((1, d)), _resident(w_mix.shape)],
        out_specs=[pl.BlockSpec((tm, w), lambda i: (i, 0)) for w in widths],
        out_shape=[jax.ShapeDtypeStruct((n, w), dt) for w, dt in zip(widths, dtypes)],
        compiler_params=_cparams(("parallel",)),
        name="inproj",
    )(x2d, gain_row, w_mix)


KV_CHUNKS = 2 * HA * DA // LANES


def _store_kv_rows(out_ref, kv, tm):
    for c in range(KV_CHUNKS):
        out_ref[pl.ds(c, tm, stride=KV_CHUNKS), :] = kv[:, c * LANES:(c + 1) * LANES]


def _inproj_prompt_kernel(x_ref, g_ref, w_ref, qa_ref, kva_ref, kva16_ref, qb_ref, kvb_ref, kvb16_ref, means_ref,
                          cqkv_ref, cg_ref, ba_ref, *, tm):
    h = _rms(x_ref[...], g_ref[...]).astype(BF16)

    def mm(lo, hi):
        return _dot(h, w_ref[:, lo:hi])

    qa_ref[...] = mm(_OFF[0], _OFF[1]).astype(BF16)
    kva = mm(_OFF[1], _OFF[3])
    kva16_ref[...] = kva.astype(BF16)
    _store_kv_rows(kva_ref, kva, tm)
    qb_ref[...] = mm(_OFF[3], _OFF[4]).astype(BF16)
    kvb = mm(_OFF[4], _OFF[6])
    kvb16_ref[...] = kvb.astype(BF16)
    _store_kv_rows(kvb_ref, kvb, tm)
    for jb in range(tm // MOBA_BLOCK):
        blk = kvb[jb * MOBA_BLOCK:(jb + 1) * MOBA_BLOCK, :HB * DB]
        means_ref[jb:jb + 1, :] = jnp.sum(blk, axis=0, keepdims=True) * (1.0 / MOBA_BLOCK)
    cqkv_ref[...] = mm(_OFF[6], _OFF[7])
    cg_ref[...] = mm(_OFF[7], _OFF[8])
    ba_ref[...] = mm(BA_COL, N_MIX_PAD)


def _inproj_prompt(x2d, gain_row, w_mix):
    n, d = x2d.shape
    tm = ROW_TILE
    assert n % tm == 0 and tm % MOBA_BLOCK == 0 and HA * DA == HB * DB
    row = lambda i: (i, 0)
    kvw = 2 * HA * DA
    out = [
        (pl.BlockSpec((tm, HA * DA), row), jax.ShapeDtypeStruct((n, HA * DA), BF16)),
        (pl.BlockSpec((tm * KV_CHUNKS, LANES), row), jax.ShapeDtypeStruct((n * KV_CHUNKS, LANES), F32)),
        (pl.BlockSpec((tm, kvw), row), jax.ShapeDtypeStruct((n, kvw), BF16)),
        (pl.BlockSpec((tm, HB * DB), row), jax.ShapeDtypeStruct((n, HB * DB), BF16)),
        (pl.BlockSpec((tm * KV_CHUNKS, LANES), row), jax.ShapeDtypeStruct((n * KV_CHUNKS, LANES), F32)),
        (pl.BlockSpec((tm, kvw), row), jax.ShapeDtypeStruct((n, kvw), BF16)),
        (pl.BlockSpec((None, tm // MOBA_BLOCK, HB * DB), lambda i: (i, 0, 0)),
         jax.ShapeDtypeStruct((n // tm, tm // MOBA_BLOCK, HB * DB), F32)),
        (pl.BlockSpec((tm, C_CONV), row), jax.ShapeDtypeStruct((n, C_CONV), F32)),
        (pl.BlockSpec((tm, HC * DV), row), jax.ShapeDtypeStruct((n, HC * DV), F32)),
        (pl.BlockSpec((tm, N_MIX_PAD - BA_COL), row), jax.ShapeDtypeStruct((n, N_MIX_PAD - BA_COL), F32)),
    ]
    return pl.pallas_call(
        functools.partial(_inproj_prompt_kernel, tm=tm),
        grid=(n // tm,),
        in_specs=[pl.BlockSpec((tm, d), row), _resident((1, d)), _resident(w_mix.shape)],
        out_specs=[o[0] for o in out],
        out_shape=[o[1] for o in out],
        compiler_params=_cparams(("parallel",)),
        name="inproj_prompt",
    )(x2d, gain_row, w_mix)


def _diff_lambda(dl, lam_init):
    s1 = jnp.sum(dl[0:1, :] * dl[1:2, :], axis=1, keepdims=True)
    s2 = jnp.sum(dl[2:3, :] * dl[3:4, :], axis=1, keepdims=True)
    return jnp.exp(s1) - jnp.exp(s2) + lam_init


def _split_halves(q):
    lane = lax.broadcasted_iota(jnp.int32, q.shape, 1)
    zero = jnp.zeros_like(q)
    return jnp.concatenate([jnp.where(lane < DA_HALF, q, zero), jnp.where(lane >= DA_HALF, q, zero)], axis=0)


def _sample_softmax_pv(s_sc, v_sc, qs, kvn_ref, n_heads, tok, slopes, scale, ts, lam=None, bias=None):
    heads = range(n_heads)
    d = qs[0].shape[1]
    kn = [_bf16_round(kvn_ref[:, h * d:(h + 1) * d]) for h in heads]
    vn = [_bf16_round(kvn_ref[:, (n_heads + h) * d:(n_heads + h + 1) * d]) for h in heads]
    fresh = [[jnp.where(tok >= j, jnp.sum(qs[h] * kn[h][j:j + 1, :], axis=-1, keepdims=True) * scale
                        - slopes[h] * (tok - j).astype(F32), NEG) for j in range(ts)] for h in heads]
    s_all = [s_sc[h] if bias is None else s_sc[h] + bias[h] for h in heads]
    m = [functools.reduce(jnp.maximum, fresh[h] + [jnp.max(s_all[h], axis=-1, keepdims=True)]) for h in heads]
    p = [jnp.exp(s_all[h] - m[h]) for h in heads]
    pf = [[jnp.exp(f - m[h]) for f in fresh[h]] for h in heads]
    inv = [1.0 / functools.reduce(jnp.add, pf[h] + [jnp.sum(p[h], axis=-1, keepdims=True)]) for h in heads]
    p = [p[h] * inv[h] for h in heads]
    pf = [[x * inv[h] for x in pf[h]] for h in heads]
    if lam is not None:
        p = [x[:ts] - lam * x[ts:] for x in p]
        pf = [[x[:ts] - lam * x[ts:] for x in pf[h]] for h in heads]
        p = [jnp.concatenate([x, x], axis=0) for x in p]
    out = [_dot(p[h].astype(BF16), v_sc[h])[:ts] for h in heads]
    return [functools.reduce(jnp.add, [out[h]] + [_bf16_round(x[:ts]) * vn[h][j:j + 1, :]
                                                   for j, x in enumerate(pf[h])]) for h in heads]


def _bf16_round(x):
    return x.astype(BF16).astype(F32)


def _top_blocks(gate, eligible, axis=1):
    idx = lax.broadcasted_iota(jnp.int32, gate.shape, axis).astype(F32)
    g = jnp.where(eligible, gate, NEG)
    sel = jnp.zeros(gate.shape, F32)
    for _ in range(MOBA_TOPK):
        mx = jnp.max(g, axis=axis, keepdims=True)
        first = jnp.min(jnp.where(g == mx, idx, float(gate.shape[axis])), axis=axis, keepdims=True)
        hit = (idx == first) & (mx > 0.5 * NEG)
        sel = jnp.where(hit, 1.0, sel)
        g = jnp.where(hit, NEG, g)
    return sel


def _lane_column(mat, n):
    lane = lax.broadcasted_iota(jnp.int32, mat.shape, 1)
    return jnp.sum(jnp.where(lane == n, mat, 0.0), axis=-1, keepdims=True)


def _tile_offsets(rows, width, tq):
    row = lax.broadcasted_iota(jnp.int32, (rows, width), 0)
    col = lax.broadcasted_iota(jnp.int32, (rows, width), 1)
    return (row & (tq - 1)) - col


ATTN_ROW_GROUP = 128


def _causal_softmax_pv(qs, kbf, vbf, bias, width, tq, scale=None, block_off=None, lam=None):
    rg = min(ATTN_ROW_GROUP, tq)
    assert tq % rg == 0
    groups = [slice(i * rg, (i + 1) * rg) for i in range(tq // rg)]
    units = [(g, q) for g in groups for q in qs]
    past = width - tq
    ss = [_dot_nt(q[g], kbf[0:width, :]) for g, q in units]
    if scale is not None:
        ss = [s * scale for s in ss]
    ss = [s + bias[g, 0:width] for s, (g, _) in zip(ss, units)]
    masked = []
    for s, (g, _) in zip(ss, units):
        pieces = []
        for j in range(past // tq):
            piece = s[:, j * tq:(j + 1) * tq]
            pieces.append(piece if block_off is None else piece + block_off[g, j:j + 1])
        own = jnp.where(_tile_offsets(rg, tq, tq) + g.start >= 0, s[:, past:], NEG)
        masked.append(jnp.concatenate(pieces + [own], axis=1) if pieces else own)
    ps = [jnp.exp(s - jnp.max(s, axis=-1, keepdims=True)) for s in masked]
    ps = [p * (1.0 / jnp.sum(p, axis=-1, keepdims=True)) for p in ps]
    if len(qs) == 2:
        ps = [ps[2 * i] - lam * ps[2 * i + 1] for i in range(len(groups))]
    os = [_dot(p.astype(BF16), vbf[0:width, :]) for p in ps]
    return jnp.concatenate(os, axis=0) if len(os) > 1 else os[0]


def _diff_prompt_kernel(q_ref, kbf, vbf, dl_ref, gain_ref, slope_ref, o_ref, bias, *, tq, nq, lam_init):
    qi = pl.program_id(2)

    @pl.when(qi == 0)
    def _():
        bias[...] = -slope_ref[:, 0:1] * _tile_offsets(tq, bias.shape[1], tq).astype(F32)

    q2 = _split_halves(q_ref[...] * (DA_HALF ** -0.5))
    lam = _diff_lambda(dl_ref[...], lam_init)

    for v in range(nq):
        @pl.when(qi == v)
        def _(v=v):
            out = _causal_softmax_pv([q2[:tq], q2[tq:]], kbf, vbf, bias, (v + 1) * tq, tq, lam=lam)
            o_ref[...] = (_rms(out, gain_ref[...]) * (1.0 - lam_init)).astype(o_ref.dtype)


def _diff_attn_prompt(q, kv, dl, gain_row, slopes, lam_init, b, t):
    n = b * t
    tq = ATTN_TQ
    assert t % tq == 0 and tq & (tq - 1) == 0
    nq = t // tq
    assert nq <= 16
    kern = functools.partial(_diff_prompt_kernel, tq=tq, nq=nq, lam_init=lam_init)
    return pl.pallas_call(
        kern,
        grid=(b, HA, nq),
        in_specs=[
            pl.BlockSpec((tq, DA), lambda bi, h, qi: (bi * nq + qi, h)),
            pl.BlockSpec((t, DA), lambda bi, h, qi: (bi, h)),
            pl.BlockSpec((t, DA), lambda bi, h, qi: (bi, HA + h)),
            pl.BlockSpec((4, DA_HALF), lambda bi, h, qi: (0, 0)),
            pl.BlockSpec((1, DA), lambda bi, h, qi: (0, 0)),
            pl.BlockSpec((None, 1, LANES), lambda bi, h, qi: (h, 0, 0)),
        ],
        out_specs=pl.BlockSpec((tq, DA), lambda bi, h, qi: (bi * nq + qi, h)),
        out_shape=jax.ShapeDtypeStruct((n, HA * DA), BF16),
        scratch_shapes=[pltpu.VMEM((tq, t), F32)],
        compiler_params=_cparams(("parallel", "parallel", "arbitrary")),
        name="diff_attn_prompt",
    )(q, kv, kv, dl, gain_row, slopes)


def _moba_prompt_kernel(q_ref, kbf, vbf, means_ref, slope_ref, o_ref, means, bias, *, tq, nb):
    qi = pl.program_id(2)
    slope = slope_ref[:, 0:1]

    @pl.when(qi == 0)
    def _():
        bias[...] = -slope * _tile_offsets(tq, bias.shape[1], tq).astype(F32)
        means[...] = jnp.zeros(means.shape, F32)
        means[0:nb, :] = means_ref[...]

    q = q_ref[...]
    scale = DB ** -0.5
    nbp = -(-nb // SUBLANES) * SUBLANES
    gate_t = _dot_nt(means[...].astype(BF16), q)[0:nbp]
    blk = lax.broadcasted_iota(jnp.int32, gate_t.shape, 0)
    sel_t = _top_blocks(gate_t, blk < qi, axis=0)
    sel = jnp.concatenate([sel_t, jnp.zeros((LANES - nbp, tq), F32)], axis=0).T
    off = jnp.where(sel > 0.5, 0.0, NEG)

    for v in range(nb):
        @pl.when(qi == v)
        def _(v=v):
            o_ref[...] = _causal_softmax_pv([q], kbf, vbf, bias, (v + 1) * tq, tq, scale, off).astype(o_ref.dtype)


def _moba_attn_prompt(q, kv, means, slopes, b, t):
    n = b * t
    tq = ATTN_TQ
    assert tq == MOBA_BLOCK and t % tq == 0 and tq & (tq - 1) == 0
    nq = t // tq
    assert nq <= 16
    kern = functools.partial(_moba_prompt_kernel, tq=tq, nb=nq)
    return pl.pallas_call(
        kern,
        grid=(b, HB, nq),
        in_specs=[
            pl.BlockSpec((tq, DB), lambda bi, h, qi: (bi * nq + qi, h)),
            pl.BlockSpec((t, DB), lambda bi, h, qi: (bi, h)),
            pl.BlockSpec((t, DB), lambda bi, h, qi: (bi, HB + h)),
            pl.BlockSpec((None, nq, DB), lambda bi, h, qi: (bi, 0, h)),
            pl.BlockSpec((None, 1, LANES), lambda bi, h, qi: (h, 0, 0)),
        ],
        out_specs=pl.BlockSpec((tq, DB), lambda bi, h, qi: (bi * nq + qi, h)),
        out_shape=jax.ShapeDtypeStruct((n, HB * DB), BF16),
        scratch_shapes=[pltpu.VMEM((LANES, DB), F32), pltpu.VMEM((tq, t), F32)],
        compiler_params=_cparams(("parallel", "parallel", "arbitrary")),
        name="moba_attn_prompt",
    )(q, kv, kv, means, slopes)


def _head_rows(pages, kv, h, n_heads, page):
    parts = [pg[pl.ds(kv * n_heads + h, page, stride=2 * n_heads), :] for pg in pages]
    return jnp.concatenate(parts, axis=0).astype(BF16)


def _diff_sample_kernel(pt_ref, q_ref, kvn_ref, dl_ref, gain_ref, *rest, n_in, ts, page, past_len, lam_init,
                        slopes):
    del pt_ref
    pages = rest[:n_in]
    o_ref = rest[n_in]
    s_sc, v_sc = rest[n_in + 1:]
    i = pl.program_id(1)
    rows = 2 * ts
    nkeys = n_in * page
    row1 = lax.broadcasted_iota(jnp.int32, (rows, 1), 0)
    tok1 = jnp.where(row1 < ts, row1, row1 - ts)
    lane = lax.broadcasted_iota(jnp.int32, (rows, nkeys), 1)
    dist = ((past_len + tok1) - (i * nkeys + lane)).astype(F32)
    q_all = q_ref[...]
    q2 = [_split_halves(q_all[:, h * DA:(h + 1) * DA] * (DA_HALF ** -0.5)).astype(BF16) for h in range(HA)]

    off = pl.multiple_of(i * nkeys, nkeys)
    heads = range(HA)
    ss = [_dot_nt(q2[h], _head_rows(pages, 0, h, HA, page)) - slopes[h] * dist for h in heads]
    for h in heads:
        s_sc[h, :, pl.ds(off, nkeys)] = ss[h]
        v_sc[h, pl.ds(off, nkeys), :] = _head_rows(pages, 1, h, HA, page)

    @pl.when(i == pl.num_programs(1) - 1)
    def _():
        lam = _diff_lambda(dl_ref[...], lam_init)
        outs = _sample_softmax_pv(s_sc, v_sc, [q2[h].astype(F32) for h in heads], kvn_ref, HA, tok1, slopes, 1.0,
                                  ts, lam)
        for h in heads:
            o_ref[:, h * DA:(h + 1) * DA] = _rms(outs[h], gain_ref[...]) * (1.0 - lam_init)


def _page_specs(layer, n_pages, n_in, rows):
    def spec(j):
        return pl.BlockSpec((None, None, rows, LANES),
                            lambda b, i, pt: (layer, pt[b * n_pages + i * n_in + j], 0, 0))
    return [spec(j) for j in range(n_in)]


def _diff_attn_sample(q3, kvn3, cache4, pt_flat, dl, gain_row, slopes, lam_init, layer, n_pages, page):
    bs, ts, _ = q3.shape
    assert 2 * ts == SUBLANES
    n_in = min(PAGES_PER_STEP, n_pages)
    assert n_pages % n_in == 0
    rows = cache4.shape[2]
    kern = functools.partial(_diff_sample_kernel, n_in=n_in, ts=ts, page=page, past_len=n_pages * page,
                             lam_init=lam_init, slopes=slopes)
    grid_spec = pltpu.PrefetchScalarGridSpec(
        num_scalar_prefetch=1,
        grid=(bs, n_pages // n_in),
        in_specs=[
            pl.BlockSpec((None, ts, HA * DA), lambda b, i, pt: (b, 0, 0)),
            pl.BlockSpec((None, ts, 2 * HA * DA), lambda b, i, pt: (b, 0, 0)),
            pl.BlockSpec((4, DA_HALF), lambda b, i, pt: (0, 0)),
            pl.BlockSpec((1, DA), lambda b, i, pt: (0, 0)),
        ] + _page_specs(layer, n_pages, n_in, rows),
        out_specs=pl.BlockSpec((None, ts, HA * DA), lambda b, i, pt: (b, 0, 0)),
        scratch_shapes=[pltpu.VMEM((HA, 2 * ts, n_pages * page), F32), pltpu.VMEM((HA, n_pages * page, DA), BF16)],
    )
    return pl.pallas_call(
        kern,
        grid_spec=grid_spec,
        out_shape=jax.ShapeDtypeStruct((bs, ts, HA * DA), F32),
        compiler_params=_cparams(("parallel", "arbitrary")),
        name="diff_attn_sample",
    )(pt_flat, q3, kvn3, dl, gain_row, *([cache4] * n_in))


def _moba_sample_kernel(pt_ref, q_ref, kvn_ref, *rest, n_in, ts, page, past_len, slopes):
    del pt_ref
    pages = rest[:n_in]
    o_ref = rest[n_in]
    means_sc, s_sc, v_sc = rest[n_in + 1:]
    i = pl.program_id(1)
    rows = SUBLANES
    nkeys = n_in * page
    row1 = lax.broadcasted_iota(jnp.int32, (rows, 1), 0)
    tok1 = jnp.where(row1 < ts, row1, row1 - ts)
    lane = lax.broadcasted_iota(jnp.int32, (rows, nkeys), 1)
    dist = ((past_len + tok1) - (i * nkeys + lane)).astype(F32)
    scale = DB ** -0.5
    ppb = MOBA_BLOCK // page
    blocks_per_step = n_in // ppb
    nb_past = past_len // MOBA_BLOCK
    q_all = q_ref[...]
    q8 = [jnp.concatenate([q_all[:, h * DB:(h + 1) * DB]] * 2, axis=0).astype(BF16) for h in range(HB)]

    @pl.when(i == 0)
    def _():
        means_sc[...] = jnp.zeros(means_sc.shape, F32)

    off = pl.multiple_of(i * nkeys, nkeys)
    boff = pl.multiple_of(i * blocks_per_step, blocks_per_step)
    heads = range(HB)
    kparts = [[pg[pl.ds(h, page, stride=2 * HB), :] for pg in pages] for h in heads]
    ss = [_dot_nt(q8[h], jnp.concatenate(kparts[h], axis=0).astype(BF16)) * scale - slopes[h] * dist for h in heads]
    for h in heads:
        s_sc[h, :, pl.ds(off, nkeys)] = ss[h]
        v_sc[h, pl.ds(off, nkeys), :] = _head_rows(pages, 1, h, HB, page)
        sums = [jnp.sum(functools.reduce(jnp.add, kparts[h][jb * ppb:(jb + 1) * ppb]), axis=0, keepdims=True)
                for jb in range(blocks_per_step)]
        means_sc[h, pl.ds(boff, blocks_per_step), :] = jnp.concatenate(sums, axis=0) * (1.0 / MOBA_BLOCK)

    @pl.when(i == pl.num_programs(1) - 1)
    def _():
        bias = []
        for h in heads:
            gate = _dot_nt(q8[h], means_sc[h].astype(BF16))
            glane = lax.broadcasted_iota(jnp.int32, gate.shape, 1)
            block_off = jnp.where(_top_blocks(gate, glane < nb_past) > 0.5, 0.0, NEG)
            bias.append(jnp.concatenate([jnp.broadcast_to(block_off[:, j:j + 1], (rows, MOBA_BLOCK))
                                         for j in range(nb_past)], axis=1))
        outs = _sample_softmax_pv(s_sc, v_sc, [q8[h].astype(F32) for h in heads], kvn_ref, HB, tok1, slopes, scale,
                                  ts, bias=bias)
        for h in heads:
            o_ref[:, h * DB:(h + 1) * DB] = outs[h]


def _moba_attn_sample(q3, kvn3, cache4, pt_flat, slopes, layer, n_pages, page):
    bs, ts, _ = q3.shape
    assert 2 * ts == SUBLANES
    past_len = n_pages * page
    assert past_len % MOBA_BLOCK == 0 and ts <= MOBA_BLOCK and past_len // MOBA_BLOCK <= LANES
    n_in = min(PAGES_PER_STEP, n_pages)
    assert n_pages % n_in == 0 and MOBA_BLOCK % page == 0 and (n_in * page) % MOBA_BLOCK == 0
    assert n_in == n_pages or (n_in * page // MOBA_BLOCK) % SUBLANES == 0
    rows = cache4.shape[2]
    kern = functools.partial(_moba_sample_kernel, n_in=n_in, ts=ts, page=page, past_len=past_len, slopes=slopes)
    grid_spec = pltpu.PrefetchScalarGridSpec(
        num_scalar_prefetch=1,
        grid=(bs, n_pages // n_in),
        in_specs=[
            pl.BlockSpec((None, ts, HB * DB), lambda b, i, pt: (b, 0, 0)),
            pl.BlockSpec((None, ts, 2 * HB * DB), lambda b, i, pt: (b, 0, 0)),
        ] + _page_specs(layer, n_pages, n_in, rows),
        out_specs=pl.BlockSpec((None, ts, HB * DB), lambda b, i, pt: (b, 0, 0)),
        scratch_shapes=[pltpu.VMEM((HB, LANES, DB), F32), pltpu.VMEM((HB, SUBLANES, past_len), F32),
                        pltpu.VMEM((HB, past_len, DB), BF16)],
    )
    return pl.pallas_call(
        kern,
        grid_spec=grid_spec,
        out_shape=jax.ShapeDtypeStruct((bs, ts, HB * DB), F32),
        compiler_params=_cparams(("parallel", "arbitrary")),
        name="moba_attn_sample",
    )(pt_flat, q3, kvn3, *([cache4] * n_in))


def _split_bf16(x, parts):
    out = []
    for _ in range(parts):
        hi = x.astype(BF16)
        out.append(hi)
        x = x - hi.astype(F32)
    return out


def _dot3(a, b):
    a_hi, a_lo = _split_bf16(a, 2)
    b_hi, b_lo = _split_bf16(b, 2)
    return _dot(a_hi, b_hi) + (_dot(a_hi, b_lo) + _dot(a_lo, b_hi))


def _unit_lower_inverses(nmats):
    c = nmats[0].shape[0]
    row = lax.broadcasted_iota(jnp.int32, (c, c), 0)
    col = lax.broadcasted_iota(jnp.int32, (c, c), 1)
    eye = (row == col).astype(F32)
    shift = 3
    same = lax.shift_right_logical(row, shift) == lax.shift_right_logical(col, shift)
    n0 = [jnp.where(same, n, 0.0) for n in nmats]
    n2 = [_dot3(a, a) for a in n0]
    n4 = [_dot3(a, a) for a in n2]
    inv = [_dot3(eye - a, eye + b) for a, b in zip(n0, n2)]
    inv = [_dot3(a, eye + b) for a, b in zip(inv, n4)]
    while (1 << shift) < c:
        shift += 1
        same2 = lax.shift_right_logical(row, shift) == lax.shift_right_logical(col, shift)
        grow = same2 & jnp.logical_not(same)
        left = [_dot3(a, jnp.where(grow, n, 0.0)) for a, n in zip(inv, nmats)]
        inv = [a - _dot3(b, a) for a, b in zip(inv, left)]
        same = same2
    return inv


def _gdn_prep_kernel(cqkv_ref, halo_ref, ba_ref, prev_ref, cw_ref, alog_ref, dtb_ref,
                     u_ref, w_ref, qd_ref, kd_ref, qk_ref, egl_ref, xbuf, *, chunk, n_chunks, t_valid):
    c = pl.program_id(1)
    rows = n_chunks * chunk
    halo = SUBLANES
    xbuf[0:halo, :] = jnp.where(c == 0, prev_ref[...], halo_ref[...])
    xbuf[halo:halo + rows, :] = cqkv_ref[...]
    conv = cw_ref[0:1, :] * xbuf[halo - 3:halo - 3 + rows, :]
    for j in range(1, CONV_W):
        conv = conv + cw_ref[j:j + 1, :] * xbuf[halo - 3 + j:halo - 3 + j + rows, :]
    conv = conv * jax.nn.sigmoid(conv)

    rowc = lax.broadcasted_iota(jnp.int32, (rows, 1), 0)
    valid = (c * rows + rowc) < t_valid
    ba = ba_ref[...]
    beta_all = jnp.where(valid, jax.nn.sigmoid(ba), 0.0)
    g_all = jnp.where(valid, -jnp.exp(alog_ref[...]) * jax.nn.softplus(ba + dtb_ref[...]), 0.0)
    ri = lax.broadcasted_iota(jnp.int32, (chunk, chunk), 0)
    ci = lax.broadcasted_iota(jnp.int32, (chunk, chunk), 1)
    tril16 = (ri >= ci).astype(BF16)
    eye16 = (ri == ci).astype(BF16)

    probs = []
    for g in range(n_chunks):
        r0 = g * chunk
        gc_all = functools.reduce(jnp.add, [_dot(tril16, p) for p in _split_bf16(g_all[r0:r0 + chunk], 3)])
        gc_t = functools.reduce(jnp.add, [_dot_tn(p, eye16) for p in _split_bf16(gc_all, 3)])
        egl_ref[r0:r0 + chunk, :] = jnp.broadcast_to(jnp.exp(gc_all[chunk - 1:chunk, :]), (chunk, LANES))
        for h in range(HC):
            qh = conv[r0:r0 + chunk, h * DK:(h + 1) * DK]
            kh = conv[r0:r0 + chunk, (HC + h) * DK:(HC + h + 1) * DK]
            vh = conv[r0:r0 + chunk, 2 * HC * DK + h * DV:2 * HC * DK + (h + 1) * DV]
            qh = qh * lax.rsqrt(jnp.sum(qh * qh, axis=-1, keepdims=True) + 1e-6) * (DK ** -0.5)
            kh = kh * lax.rsqrt(jnp.sum(kh * kh, axis=-1, keepdims=True) + 1e-6)
            beta = _lane_column(beta_all[r0:r0 + chunk], h)
            gc = _lane_column(gc_all, HC + h)
            gc_row = gc_t[HC + h:HC + h + 1, :]
            probs.append(dict(r0=r0, h=h, qh=qh, kh=kh, vh=vh, beta=beta, gc=gc, diff=gc - gc_row,
                              gc_last=gc_row[:, chunk - 1:chunk], kbeta=kh * beta, k16=kh.astype(BF16)))
    nmats = [_dot_nt(p["kbeta"].astype(BF16), p["k16"]) * jnp.exp(jnp.where(ri > ci, p["diff"], NEG)) for p in probs]
    qks = [_dot_nt(p["qh"].astype(BF16), p["k16"]) * jnp.exp(jnp.where(ri >= ci, p["diff"], NEG)) for p in probs]
    invs = _unit_lower_inverses(nmats)
    uws = []
    for p, inv in zip(probs, invs):
        egc = jnp.exp(p["gc"])
        rhs = jnp.concatenate([(p["vh"] * p["beta"]).astype(BF16), (p["kbeta"] * egc).astype(BF16)], axis=1)
        uws.append(_dot(inv.astype(BF16), rhs))
        p["egc"] = egc
    for p, uw, qk in zip(probs, uws, qks):
        rs = slice(p["r0"], p["r0"] + chunk)
        hs = slice(p["h"] * DK, (p["h"] + 1) * DK)
        u_ref[rs, hs] = uw[:, :DV]
        w_ref[rs, hs] = uw[:, DV:].astype(BF16)
        qk_ref[rs, p["h"] * chunk:(p["h"] + 1) * chunk] = qk.astype(BF16)
        qd_ref[rs, hs] = (p["qh"] * p["egc"]).astype(BF16)
        kd_ref[rs, hs] = (p["kh"] * jnp.exp(p["gc_last"] - p["gc"])).astype(BF16)


def _gdn_scan_kernel(u_ref, w_ref, qd_ref, kd_ref, qk_ref, egl_ref, cg_ref, s0_ref, gain_ref, o_ref, s_ref,
                     *, n_seq, chunk):
    c = pl.program_id(1)

    @pl.when(c == 0)
    def _():
        s_ref[...] = s0_ref[...]

    chains = [(bb, h, slice(h * DK, (h + 1) * DK)) for bb in range(n_seq) for h in range(HC)]
    s_old = [s_ref[bb, h] for bb, h, _ in chains]
    s16 = [s.astype(BF16) for s in s_old]
    ws = [_dot(jnp.concatenate([w_ref[bb, :, hs], qd_ref[bb, :, hs]], axis=0), s)
          for (bb, _, hs), s in zip(chains, s16)]
    v16 = [(u_ref[bb, :, hs] - x[:chunk]).astype(BF16) for (bb, _, hs), x in zip(chains, ws)]
    o = [x[chunk:] + _dot(qk_ref[bb, :, h * chunk:(h + 1) * chunk], v) for (bb, h, _), x, v in zip(chains, ws, v16)]
    kv = [_dot_tn(kd_ref[bb, :, hs], v) for (bb, _, hs), v in zip(chains, v16)]
    for (bb, h, hs), s, upd, out in zip(chains, s_old, kv, o):
        decay = _lane_column(egl_ref[bb, 0:1, :], HC + h)
        s_ref[bb, h] = s * decay + upd
        gate = cg_ref[bb, :, hs]
        o_ref[bb, :, hs] = (_rms(out, gain_ref[...]) * (gate * jax.nn.sigmoid(gate))).astype(o_ref.dtype)


def _gdn(cqkv, cg, ba, prev8, s0, conv_w8, alog_row, dtb_row, gain_row, b, t, t_valid):
    chunk = GDN_CHUNK
    assert t % chunk == 0 and DK == DV
    nc = t // chunk
    n = b * t
    g = 2 if nc % 2 == 0 else 1
    rows = g * chunk
    steps = nc // g
    hb = SUBLANES
    prep = functools.partial(_gdn_prep_kernel, chunk=chunk, n_chunks=g, t_valid=t_valid)
    blk = lambda width: pl.BlockSpec((rows, width), lambda bi, c: (bi * steps + c, 0))
    u, w, qd, kd, qk, egl = pl.pallas_call(
        prep,
        grid=(b, steps),
        in_specs=[
            blk(C_CONV),
            pl.BlockSpec((hb, C_CONV), lambda bi, c: (jnp.maximum((bi * t + c * rows) // hb - 1, 0), 0)),
            blk(LANES),
            pl.BlockSpec((None, hb, C_CONV), lambda bi, c: (bi, 0, 0)),
            pl.BlockSpec((SUBLANES, C_CONV), lambda bi, c: (0, 0)),
            pl.BlockSpec((1, LANES), lambda bi, c: (0, 0)),
            pl.BlockSpec((1, LANES), lambda bi, c: (0, 0)),
        ],
        out_specs=[blk(HC * DV), blk(HC * DK), blk(HC * DK), blk(HC * DK), blk(HC * chunk), blk(LANES)],
        out_shape=[jax.ShapeDtypeStruct((n, HC * DV), F32), jax.ShapeDtypeStruct((n, HC * DK), BF16),
                   jax.ShapeDtypeStruct((n, HC * DK), BF16), jax.ShapeDtypeStruct((n, HC * DK), BF16),
                   jax.ShapeDtypeStruct((n, HC * chunk), BF16), jax.ShapeDtypeStruct((n, LANES), F32)],
        scratch_shapes=[pltpu.VMEM((rows + hb, C_CONV), F32)],
        compiler_params=_cparams(("parallel", "parallel")),
        name="gdn_prep",
    )(cqkv, cqkv, ba, prev8, conv_w8, alog_row, dtb_row)

    n_seq = 4 if b % 4 == 0 else (2 if b % 2 == 0 else 1)
    scan = functools.partial(_gdn_scan_kernel, n_seq=n_seq, chunk=chunk)
    seq = lambda width: pl.BlockSpec((n_seq, chunk, width), lambda bg, c: (bg, c, 0))
    state = pl.BlockSpec((n_seq, HC, DK, DV), lambda bg, c: (bg, 0, 0, 0))
    as3 = lambda a: a.reshape(b, t, a.shape[-1])
    out, s_new = pl.pallas_call(
        scan,
        grid=(b // n_seq, nc),
        in_specs=[seq(HC * DV), seq(HC * DK), seq(HC * DK), seq(HC * DK), seq(HC * chunk), seq(LANES),
                  seq(HC * DV), state, pl.BlockSpec((1, DV), lambda bg, c: (0, 0))],
        out_specs=[seq(HC * DV), state],
        out_shape=[jax.ShapeDtypeStruct((b, t, HC * DV), BF16), jax.ShapeDtypeStruct((b, HC, DK, DV), F32)],
        compiler_params=_cparams(("parallel", "arbitrary")),
        name="gdn_scan",
    )(as3(u), as3(w), as3(qd), as3(kd), as3(qk), as3(egl), as3(cg), s0, gain_row)
    return out.reshape(n, HC * DV), s_new


def _route(logits):
    lane = lax.broadcasted_iota(jnp.int32, logits.shape, 1).astype(F32)

    def argmax_first(x):
        mx = jnp.max(x, axis=-1, keepdims=True)
        return mx, jnp.min(jnp.where(x == mx, lane, float(LANES)), axis=-1, keepdims=True)

    lg = jnp.where(lane < N_GROUPS, logits, NEG)
    mg, g_top = argmax_first(lg)
    pg_top = 1.0 / jnp.sum(jnp.exp(lg - mg), axis=-1, keepdims=True)
    lo = N_GROUPS + g_top * EXPERTS_PER_GROUP
    le = jnp.where((lane >= lo) & (lane < lo + EXPERTS_PER_GROUP), logits, NEG)
    m1, i1 = argmax_first(le)
    m2, i2 = argmax_first(jnp.where(lane == i1, NEG, le))
    se = jnp.sum(jnp.exp(le - m1), axis=-1, keepdims=True)
    pe1 = 1.0 / se
    pe2 = jnp.exp(m2 - m1) / se
    w1 = pg_top * pe1 / (pe1 + pe2)
    w2 = pg_top * pe2 / (pe1 + pe2)
    out = jnp.where(lane == 0, i1 - N_GROUPS, 0.0)
    out = jnp.where(lane == 1, i2 - N_GROUPS, out)
    out = jnp.where(lane == 2, w1, out)
    chosen = jnp.where((lane == i1) | (lane == i2), 1.0, 0.0)
    return jnp.where(lane == 3, w2, out), jnp.sum(chosen, axis=0, keepdims=True)


def _merge_kernel(x_ref, a_ref, b_ref, c_ref, gm_ref, wg_ref, wb_ref, wo_ref, gf_ref, wr_ref,
                  x1_ref, h2_ref, route_ref, counts_ref):
    x = x_ref[...]
    d = x.shape[-1]
    h = _rms(x, gm_ref[...]).astype(BF16)
    mix = None
    for n, br in enumerate((a_ref, b_ref, c_ref)):
        term = jax.nn.sigmoid(_dot(h, wg_ref[:, n * d:(n + 1) * d])) * _dot(br[...], wb_ref[n])
        mix = term if mix is None else mix + term
    x1 = x + _dot(mix.astype(BF16), wo_ref[...])
    x1_ref[...] = x1
    h2 = _rms(x1, gf_ref[...]).astype(BF16)
    h2_ref[...] = h2
    route, counts = _route(_dot(h2, wr_ref[...]))
    route_ref[...] = route
    counts_ref[...] = jnp.broadcast_to(counts, counts_ref.shape)


def _merge(x2d, out_a, out_b, out_c, gm_row, w_gate, w_branch, w_out, gf_row, w_router):
    n, d = x2d.shape
    tm = _row_tile(n, ROW_TILE)
    row = lambda i: (i, 0)
    return pl.pallas_call(
        _merge_kernel,
        grid=(n // tm,),
        in_specs=[
            pl.BlockSpec((tm, d), row), pl.BlockSpec((tm, BRANCH_W), row), pl.BlockSpec((tm, BRANCH_W), row),
            pl.BlockSpec((tm, BRANCH_W), row), _resident((1, d)), _resident(w_gate.shape),
            _resident(w_branch.shape), _resident(w_out.shape), _resident((1, d)), _resident(w_router.shape),
        ],
        out_specs=[pl.BlockSpec((tm, d), row), pl.BlockSpec((tm, d), row), pl.BlockSpec((tm, LANES), row),
                   pl.BlockSpec((None, SUBLANES, LANES), lambda i: (i, 0, 0))],
        out_shape=[jax.ShapeDtypeStruct((n, d), F32), jax.ShapeDtypeStruct((n, d), BF16),
                   jax.ShapeDtypeStruct((n, LANES), F32), jax.ShapeDtypeStruct((n // tm, SUBLANES, LANES), F32)],
        compiler_params=_cparams(("parallel",)),
        name="merge",
    )(x2d, out_a, out_b, out_c, gm_row, w_gate, w_branch, w_out, gf_row, w_router)


def _moe_kernel(te_ref, xs_ref, rw_ref, wg_ref, wu_ref, wd_ref, y_ref):
    del te_ref
    x = xs_ref[...]
    g = _dot(x, wg_ref[...])
    a = (g * jax.nn.sigmoid(g)) * _dot(x, wu_ref[...]) * rw_ref[...]
    y_ref[...] = _dot(a.astype(BF16), wd_ref[...])


def _moe_experts(xs, row_w, tile_expert, w_g, w_u, w_d):
    m, d = xs.shape
    f = w_g.shape[-1]
    grid_spec = pltpu.PrefetchScalarGridSpec(
        num_scalar_prefetch=1,
        grid=(m // MOE_TILE,),
        in_specs=[
            pl.BlockSpec((MOE_TILE, d), lambda i, te: (i, 0)),
            pl.BlockSpec((MOE_TILE, 1), lambda i, te: (i, 0)),
            pl.BlockSpec((None, d, f), lambda i, te: (te[i], 0, 0)),
            pl.BlockSpec((None, d, f), lambda i, te: (te[i], 0, 0)),
            pl.BlockSpec((None, f, d), lambda i, te: (te[i], 0, 0)),
        ],
        out_specs=pl.BlockSpec((MOE_TILE, d), lambda i, te: (i, 0)),
    )
    return pl.pallas_call(
        _moe_kernel,
        grid_spec=grid_spec,
        out_shape=jax.ShapeDtypeStruct((m, d), F32),
        compiler_params=_cparams(("arbitrary",)),
        name="moe_experts",
    )(tile_expert, xs, row_w, w_g, w_u, w_d)


def _combine_kernel(x_ref, y1_ref, y2_ref, gain_ref, o_ref, *, final_norm):
    x = x_ref[...] + (y1_ref[...] + y2_ref[...])
    o_ref[...] = _rms(x, gain_ref[...]) if final_norm else x


def _combine(x1, y1, y2, gain_row, final_norm):
    n, d = x1.shape
    tm = _row_tile(n, ROW_TILE)
    row = lambda i: (i, 0)
    return pl.pallas_call(
        functools.partial(_combine_kernel, final_norm=final_norm),
        grid=(n // tm,),
        in_specs=[pl.BlockSpec((tm, d), row), pl.BlockSpec((tm, d), row), pl.BlockSpec((tm, d), row),
                  _resident((1, d))],
        out_specs=pl.BlockSpec((tm, d), row),
        out_shape=jax.ShapeDtypeStruct((n, d), F32),
        compiler_params=_cparams(("parallel",)),
        name="moe_combine",
    )(x1, y1, y2, gain_row)


def _dispatch_tables(route, counts, tile):
    n = route.shape[0]
    eid = route[:, 0:2].astype(jnp.int32).reshape(-1)
    wsel = route[:, 2:4].reshape(-1)
    n_assign = 2 * n
    m_pad = -(-(n_assign + N_EXPERTS * (tile - 1)) // tile) * tile
    n_tiles = m_pad // tile
    order = jnp.argsort(eid, stable=True).astype(jnp.int32)
    padded = (counts + tile - 1) // tile * tile
    ends_p = jnp.cumsum(padded)
    starts_p = ends_p - padded
    starts = jnp.cumsum(counts) - counts
    tile_start = jnp.arange(n_tiles, dtype=jnp.int32) * tile
    tile_expert = jnp.sum((ends_p[None, :] <= tile_start[:, None]).astype(jnp.int32), axis=1)
    tile_expert = jnp.minimum(tile_expert, N_EXPERTS - 1)
    base = starts[tile_expert] - starts_p[tile_expert] + tile_start
    limit = starts[tile_expert] + counts[tile_expert]
    sorted_pos = base[:, None] + jnp.arange(tile, dtype=jnp.int32)[None, :]
    valid = (sorted_pos < limit[:, None]).reshape(-1)
    assign = order[jnp.clip(sorted_pos.reshape(-1), 0, n_assign - 1)]
    row_src = jnp.where(valid, assign // 2, 0)
    row_w = jnp.where(valid, wsel[assign], 0.0)
    key = jnp.where(valid, assign, n_assign + jnp.arange(m_pad, dtype=jnp.int32))
    slot_row = jnp.argsort(key)[:n_assign].astype(jnp.int32).reshape(n, 2)
    return row_src, row_w.reshape(m_pad, 1), slot_row, tile_expert


def _moe_dispatch(h2, route, tile_counts):
    counts = jnp.sum(tile_counts[:, 0, N_GROUPS:N_GROUPS + N_EXPERTS], axis=0).astype(jnp.int32)
    row_src, row_w, slot_row, tile_expert = _dispatch_tables(route, counts, MOE_TILE)
    return h2[row_src], row_w, slot_row, tile_expert


def _moe_finish(x1, dispatched, w_g, w_u, w_d, gain_row, final_norm):
    xs, row_w, slot_row, tile_expert = dispatched
    ys = _moe_experts(xs, row_w, tile_expert, w_g, w_u, w_d)
    return _combine(x1, ys[slot_row[:, 0]], ys[slot_row[:, 1]], gain_row, final_norm)


def _lane_row(vec, offset):
    return jnp.zeros((1, LANES), F32).at[0, offset:offset + vec.shape[0]].set(vec.astype(F32))


def _layer_weights(l, w_in, norm_mixer, diff_lambda, diff_norm_gain, gdn_conv_w, gdn_a_log, gdn_dt_bias,
                   gdn_norm_gain, w_branch, w_out, norm_ffn, router_group, router_expert, expert_w_gate,
                   expert_w_up, expert_w_down):
    d = w_in.shape[1]
    w_mix = jnp.pad(w_in[l, :, :N_MIX_COLS], ((0, 0), (0, N_MIX_PAD - N_MIX_COLS))).astype(BF16)
    w_router = jnp.concatenate([router_group[l], router_expert[l]], axis=1)
    w_router = jnp.pad(w_router, ((0, 0), (0, LANES - w_router.shape[1]))).astype(BF16)
    return dict(
        gm=norm_mixer[l].reshape(1, d), w_mix=w_mix, w_gate=w_in[l, :, N_MIX_COLS:].astype(BF16),
        dl=diff_lambda[l].astype(F32), dgain=diff_norm_gain[l].reshape(1, DA).astype(F32),
        conv_w=jnp.pad(gdn_conv_w[l].astype(F32), ((0, SUBLANES - CONV_W), (0, 0))),
        alog=_lane_row(gdn_a_log[l], HC), dtb=_lane_row(gdn_dt_bias[l], HC),
        ggain=gdn_norm_gain[l].reshape(1, DV).astype(F32),
        w_branch=w_branch[l].astype(BF16), w_out=w_out[l].astype(BF16), gf=norm_ffn[l].reshape(1, d),
        w_router=w_router, w_g=expert_w_gate[l].astype(BF16), w_u=expert_w_up[l].astype(BF16),
        w_d=expert_w_down[l].astype(BF16),
    )


def _slope_rows(slopes):
    return jnp.broadcast_to(jnp.asarray(slopes, F32)[:, None, None], (len(slopes), 1, LANES))


def _prompt_layer(x2d, lw, lam_init, b, t):
    slopes_a, slopes_b = _alibi_slopes()
    q_a, kv_a, kva16, q_b, kv_b, kvb16, means, cqkv, cg, ba = _inproj_prompt(x2d, lw["gm"], lw["w_mix"])
    out_a = _diff_attn_prompt(q_a, kva16, lw["dl"], lw["dgain"], _slope_rows(slopes_a), lam_init, b, t)
    out_b = _moba_attn_prompt(q_b, kvb16, means.reshape(b, t // MOBA_BLOCK, HB * DB), _slope_rows(slopes_b), b, t)
    prev8 = jnp.zeros((b, SUBLANES, C_CONV), F32)
    s0 = jnp.zeros((b, HC, DK, DV), F32)
    out_c, s_new = _gdn(cqkv, cg, ba, prev8, s0, lw["conv_w"], lw["alog"], lw["dtb"], lw["ggain"], b, t, t)
    x1, h2, route, counts = _merge(x2d, out_a, out_b, out_c, lw["gm"], lw["w_gate"], lw["w_branch"], lw["w_out"],
                                   lw["gf"], lw["w_router"])
    conv_new = cqkv.reshape(b, t, C_CONV)[:, t - (CONV_W - 1):]
    return (x1, _moe_dispatch(h2, route, counts)), kv_a, kv_b, s_new, conv_new


def _layer_finish(pending, lw, final_gain):
    x1, dispatched = pending
    gain = lw["gf"] if final_gain is None else final_gain
    return _moe_finish(x1, dispatched, lw["w_g"], lw["w_u"], lw["w_d"], gain, final_gain is not None)


def _sample_layer(x2d, lw, lam_init, bs, ts, layer, cache_a, cache_b, pt_flat, n_pages, page, conv_prev, s_prev):
    slopes_a, slopes_b = _alibi_slopes()
    n = bs * ts
    q_a, kv_a, q_b, kv_b, cqkv, cg, ba = _inproj(x2d, lw["gm"], lw["w_mix"], F32)
    rows = page * 2 * HA
    out_a = _diff_attn_sample(q_a.reshape(bs, ts, -1), kv_a.reshape(bs, ts, -1),
                              cache_a.reshape(cache_a.shape[0], cache_a.shape[1], rows, DA), pt_flat,
                              lw["dl"], lw["dgain"], slopes_a, lam_init, layer, n_pages, page)
    out_b = _moba_attn_sample(q_b.reshape(bs, ts, -1), kv_b.reshape(bs, ts, -1),
                              cache_b.reshape(cache_b.shape[0], cache_b.shape[1], rows, DB), pt_flat,
                              slopes_b, layer, n_pages, page)
    tpad = GDN_CHUNK

    def pad_t(a):
        return jnp.pad(a.reshape(bs, ts, -1), ((0, 0), (0, tpad - ts), (0, 0))).reshape(bs * tpad, -1)

    prev8 = jnp.pad(conv_prev.astype(F32), ((0, 0), (SUBLANES - (CONV_W - 1), 0), (0, 0)))
    out_c, s_new = _gdn(pad_t(cqkv), pad_t(cg), pad_t(ba), prev8, s_prev.astype(F32), lw["conv_w"], lw["alog"],
                        lw["dtb"], lw["ggain"], bs, tpad, ts)
    out_c = out_c.reshape(bs, tpad, -1)[:, :ts].reshape(n, -1)
    x1, h2, route, counts = _merge(x2d, out_a.reshape(n, -1).astype(BF16), out_b.reshape(n, -1).astype(BF16), out_c,
                                   lw["gm"], lw["w_gate"], lw["w_branch"], lw["w_out"], lw["gf"], lw["w_router"])
    c_in = jnp.concatenate([conv_prev.astype(F32), cqkv.reshape(bs, ts, C_CONV)], axis=1)
    return (x1, _moe_dispatch(h2, route, counts)), kv_a, kv_b, s_new, c_in[:, ts:]


def kernel(x_prompt, x_sample, cache_diff_kv, cache_moba_kv, state_gdn, state_conv, page_table, norm_mixer, w_in, diff_lambda, diff_norm_gain, gdn_conv_w, gdn_a_log, gdn_dt_bias, gdn_norm_gain, w_branch, w_out, norm_ffn, router_group, router_expert, expert_w_gate, expert_w_up, expert_w_down, norm_final):
    bp, tp, d = x_prompt.shape
    bs, ts, _ = x_sample.shape
    depth = w_in.shape[0]
    n_pages = page_table.shape[1]
    page = cache_diff_kv.shape[2]
    pt_flat = page_table.reshape(-1).astype(jnp.int32)
    final_gain = norm_final.reshape(1, d)

    xp = x_prompt.reshape(bp * tp, d)
    xs = x_sample.reshape(bs * ts, d)
    outs_p = [[], [], [], []]
    outs_s = [[], [], [], []]
    for l in range(depth):
        lw = _layer_weights(l, w_in, norm_mixer, diff_lambda, diff_norm_gain, gdn_conv_w, gdn_a_log, gdn_dt_bias,
                            gdn_norm_gain, w_branch, w_out, norm_ffn, router_group, router_expert, expert_w_gate,
                            expert_w_up, expert_w_down)
        lam_init = 0.8 - 0.6 * math.exp(-0.3 * l)
        fg = final_gain if l == depth - 1 else None
        pend_p, kva, kvb, sg, sc = _prompt_layer(xp, lw, lam_init, bp, tp)
        for acc, v in zip(outs_p, (kva.reshape(bp, tp, 2, HA, DA), kvb.reshape(bp, tp, 2, HB, DB), sg, sc)):
            acc.append(v)
        pend_s, kva, kvb, sg, sc = _sample_layer(xs, lw, lam_init, bs, ts, l, cache_diff_kv, cache_moba_kv, pt_flat,
                                                 n_pages, page, state_conv[l], state_gdn[l])
        for acc, v in zip(outs_s, (kva.reshape(bs, ts, 2, HA, DA), kvb.reshape(bs, ts, 2, HB, DB),
                                   sg.astype(state_gdn.dtype), sc)):
            acc.append(v)
        xp = _layer_finish(pend_p, lw, fg)
        xs = _layer_finish(pend_s, lw, fg)
    return (xp.reshape(bp, tp, d), xs.reshape(bs, ts, d),
            jnp.stack(outs_p[0]), jnp.stack(outs_p[1]), jnp.stack(outs_p[2]), jnp.stack(outs_p[3]),
            jnp.stack(outs_s[0]), jnp.stack(outs_s[1]), jnp.stack(outs_s[2]), jnp.stack(outs_s[3]))
```

```python
import functools
import math

import jax
import jax.numpy as jnp
from jax import lax
from jax.experimental import pallas as pl
from jax.experimental.pallas import tpu as pltpu

HA, DA = 4, 128
DA_HALF = DA // 2
HB, DB = 4, 128
MOBA_BLOCK = 256
MOBA_TOPK = 3
HC, DK, DV = 4, 128, 128
CONV_W = 4
GDN_CHUNK = 64
C_CONV = HC * (2 * DK + DV)
N_BRANCH = 3
BRANCH_W = HA * DA
N_GROUPS = 4
EXPERTS_PER_GROUP = 8
N_EXPERTS = N_GROUPS * EXPERTS_PER_GROUP
D_EXPERT = 256
RMS_EPS = 1e-6

LANES = 128
SUBLANES = 8
VMEM_LIMIT_BYTES = 56 * 1024 * 1024

NEG = -1e30
F32 = jnp.float32
BF16 = jnp.bfloat16

_SPLITS = (HA * DA, HA * DA, HA * DA, HB * DB, HB * DB, HB * DB, C_CONV, HC * DV, HC, HC)
_OFF = [0]
for _s in _SPLITS:
    _OFF.append(_OFF[-1] + _s)
N_MIX_COLS = _OFF[-1]
N_MIX_PAD = -(-N_MIX_COLS // LANES) * LANES
BA_COL = _OFF[8]

ROW_TILE = 512
ATTN_TQ = 256
MOE_TILE = 512
PAGES_PER_STEP = 16


def _alibi_slopes():
    n = HA + HB
    s = [2.0 ** (-8.0 * i / n) for i in range(1, n + 1)]
    return s[0::2], s[1::2]


def _cparams(semantics):
    return pltpu.CompilerParams(dimension_semantics=semantics, vmem_limit_bytes=VMEM_LIMIT_BYTES)


def _row_tile(n, pref):
    return pref if n % pref == 0 else n


def _resident(shape):
    zeros = (0,) * len(shape)
    return pl.BlockSpec(shape, lambda *_: zeros, pipeline_mode=pl.Buffered(1))


def _rms(x, gain_row):
    return x * lax.rsqrt(jnp.mean(x * x, axis=-1, keepdims=True) + RMS_EPS) * gain_row


def _dot(a, b, **kw):
    return jnp.dot(a, b, preferred_element_type=F32, **kw)


def _dot_nt(a, b, **kw):
    return lax.dot_general(a, b, (((1,), (1,)), ((), ())), preferred_element_type=F32, **kw)


def _dot_tn(a, b, **kw):
    return lax.dot_general(a, b, (((0,), (0,)), ((), ())), preferred_element_type=F32, **kw)


def _inproj_kernel(x_ref, g_ref, w_ref, qa_ref, kva_ref, qb_ref, kvb_ref, cqkv_ref, cg_ref, ba_ref):
    h = _rms(x_ref[...], g_ref[...]).astype(BF16)

    def mm(lo, hi):
        return _dot(h, w_ref[:, lo:hi])

    qa_ref[...] = mm(_OFF[0], _OFF[1]).astype(qa_ref.dtype)
    kva_ref[...] = mm(_OFF[1], _OFF[3])
    qb_ref[...] = mm(_OFF[3], _OFF[4]).astype(qb_ref.dtype)
    kvb_ref[...] = mm(_OFF[4], _OFF[6])
    cqkv_ref[...] = mm(_OFF[6], _OFF[7])
    cg_ref[...] = mm(_OFF[7], _OFF[8])
    ba_ref[...] = mm(BA_COL, N_MIX_PAD)


def _inproj(x2d, gain_row, w_mix, q_dtype):
    n, d = x2d.shape
    tm = _row_tile(n, ROW_TILE)
    widths = (HA * DA, 2 * HA * DA, HB * DB, 2 * HB * DB, C_CONV, HC * DV, N_MIX_PAD - BA_COL)
    dtypes = (q_dtype, F32, q_dtype, F32, F32, F32, F32)
    return pl.pallas_call(
        _inproj_kernel,
        grid=(n // tm,),
        in_specs=[pl.BlockSpec((tm, d), lambda i: (i, 0)), _resident((1, d)), _resident(w_mix.shape)],
        out_specs=[pl.BlockSpec((tm, w), lambda i: (i, 0)) for w in widths],
        out_shape=[jax.ShapeDtypeStruct((n, w), dt) for w, dt in zip(widths, dtypes)],
        compiler_params=_cparams(("parallel",)),
        name="inproj",
    )(x2d, gain_row, w_mix)


KV_CHUNKS = 2 * HA * DA // LANES


def _store_kv_rows(out_ref, kv, tm):
    for c in range(KV_CHUNKS):
        out_ref[pl.ds(c, tm, stride=KV_CHUNKS), :] = kv[:, c * LANES:(c + 1) * LANES]


def _inproj_prompt_kernel(x_ref, g_ref, w_ref, qa_ref, kva_ref, kva16_ref, qb_ref, kvb_ref, kvb16_ref, means_ref,
                          cqkv_ref, cg_ref, ba_ref, *, tm):
    h = _rms(x_ref[...], g_ref[...]).astype(BF16)

    def mm(lo, hi):
        return _dot(h, w_ref[:, lo:hi])

    qa_ref[...] = mm(_OFF[0], _OFF[1]).astype(BF16)
    kva = mm(_OFF[1], _OFF[3])
    kva16_ref[...] = kva.astype(BF16)
    _store_kv_rows(kva_ref, kva, tm)
    qb_ref[...] = mm(_OFF[3], _OFF[4]).astype(BF16)
    kvb = mm(_OFF[4], _OFF[6])
    kvb16_ref[...] = kvb.astype(BF16)
    _store_kv_rows(kvb_ref, kvb, tm)
    for jb in range(tm // MOBA_BLOCK):
        blk = kvb[jb * MOBA_BLOCK:(jb + 1) * MOBA_BLOCK, :HB * DB]
        means_ref[jb:jb + 1, :] = jnp.sum(blk, axis=0, keepdims=True) * (1.0 / MOBA_BLOCK)
    cqkv_ref[...] = mm(_OFF[6], _OFF[7])
    cg_ref[...] = mm(_OFF[7], _OFF[8])
    ba_ref[...] = mm(BA_COL, N_MIX_PAD)


def _inproj_prompt(x2d, gain_row, w_mix):
    n, d = x2d.shape
    tm = ROW_TILE
    assert n % tm == 0 and tm % MOBA_BLOCK == 0 and HA * DA == HB * DB
    row = lambda i: (i, 0)
    kvw = 2 * HA * DA
    out = [
        (pl.BlockSpec((tm, HA * DA), row), jax.ShapeDtypeStruct((n, HA * DA), BF16)),
        (pl.BlockSpec((tm * KV_CHUNKS, LANES), row), jax.ShapeDtypeStruct((n * KV_CHUNKS, LANES), F32)),
        (pl.BlockSpec((tm, kvw), row), jax.ShapeDtypeStruct((n, kvw), BF16)),
        (pl.BlockSpec((tm, HB * DB), row), jax.ShapeDtypeStruct((n, HB * DB), BF16)),
        (pl.BlockSpec((tm * KV_CHUNKS, LANES), row), jax.ShapeDtypeStruct((n * KV_CHUNKS, LANES), F32)),
        (pl.BlockSpec((tm, kvw), row), jax.ShapeDtypeStruct((n, kvw), BF16)),
        (pl.BlockSpec((None, tm // MOBA_BLOCK, HB * DB), lambda i: (i, 0, 0)),
         jax.ShapeDtypeStruct((n // tm, tm // MOBA_BLOCK, HB * DB), F32)),
        (pl.BlockSpec((tm, C_CONV), row), jax.ShapeDtypeStruct((n, C_CONV), F32)),
        (pl.BlockSpec((tm, HC * DV), row), jax.ShapeDtypeStruct((n, HC * DV), F32)),
        (pl.BlockSpec((tm, N_MIX_PAD - BA_COL), row), jax.ShapeDtypeStruct((n, N_MIX_PAD - BA_COL), F32)),
    ]
    return pl.pallas_call(
        functools.partial(_inproj_prompt_kernel, tm=tm),
        grid=(n // tm,),
        in_specs=[pl.BlockSpec((tm, d), row), _resident((1, d)), _resident(w_mix.shape)],
        out_specs=[o[0] for o in out],
        out_shape=[o[1] for o in out],
        compiler_params=_cparams(("parallel",)),
        name="inproj_prompt",
    )(x2d, gain_row, w_mix)


def _diff_lambda(dl, lam_init):
    s1 = jnp.sum(dl[0:1, :] * dl[1:2, :], axis=1, keepdims=True)
    s2 = jnp.sum(dl[2:3, :] * dl[3:4, :], axis=1, keepdims=True)
    return jnp.exp(s1) - jnp.exp(s2) + lam_init


def _split_halves(q):
    lane = lax.broadcasted_iota(jnp.int32, q.shape, 1)
    zero = jnp.zeros_like(q)
    return jnp.concatenate([jnp.where(lane < DA_HALF, q, zero), jnp.where(lane >= DA_HALF, q, zero)], axis=0)


def _sample_softmax_pv(s_sc, v_sc, qs, kvn_ref, n_heads, tok, slopes, scale, ts, lam=None, bias=None):
    heads = range(n_heads)
    d = qs[0].shape[1]
    kn = [_bf16_round(kvn_ref[:, h * d:(h + 1) * d]) for h in heads]
    vn = [_bf16_round(kvn_ref[:, (n_heads + h) * d:(n_heads + h + 1) * d]) for h in heads]
    fresh = [[jnp.where(tok >= j, jnp.sum(qs[h] * kn[h][j:j + 1, :], axis=-1, keepdims=True) * scale
                        - slopes[h] * (tok - j).astype(F32), NEG) for j in range(ts)] for h in heads]
    s_all = [s_sc[h] if bias is None else s_sc[h] + bias[h] for h in heads]
    m = [functools.reduce(jnp.maximum, fresh[h] + [jnp.max(s_all[h], axis=-1, keepdims=True)]) for h in heads]
    p = [jnp.exp(s_all[h] - m[h]) for h in heads]
    pf = [[jnp.exp(f - m[h]) for f in fresh[h]] for h in heads]
    inv = [1.0 / functools.reduce(jnp.add, pf[h] + [jnp.sum(p[h], axis=-1, keepdims=True)]) for h in heads]
    p = [p[h] * inv[h] for h in heads]
    pf = [[x * inv[h] for x in pf[h]] for h in heads]
    if lam is not None:
        p = [x[:ts] - lam * x[ts:] for x in p]
        pf = [[x[:ts] - lam * x[ts:] for x in pf[h]] for h in heads]
        p = [jnp.concatenate([x, x], axis=0) for x in p]
    out = [_dot(p[h].astype(BF16), v_sc[h])[:ts] for h in heads]
    return [functools.reduce(jnp.add, [out[h]] + [_bf16_round(x[:ts]) * vn[h][j:j + 1, :]
                                                   for j, x in enumerate(pf[h])]) for h in heads]


def _bf16_round(x):
    return x.astype(BF16).astype(F32)


def _top_blocks(gate, eligible, axis=1):
    idx = lax.broadcasted_iota(jnp.int32, gate.shape, axis).astype(F32)
    g = jnp.where(eligible, gate, NEG)
    sel = jnp.zeros(gate.shape, F32)
    for _ in range(MOBA_TOPK):
        mx = jnp.max(g, axis=axis, keepdims=True)
        first = jnp.min(jnp.where(g == mx, idx, float(gate.shape[axis])), axis=axis, keepdims=True)
        hit = (idx == first) & (mx > 0.5 * NEG)
        sel = jnp.where(hit, 1.0, sel)
        g = jnp.where(hit, NEG, g)
    return sel


def _lane_column(mat, n):
    lane = lax.broadcasted_iota(jnp.int32, mat.shape, 1)
    return jnp.sum(jnp.where(lane == n, mat, 0.0), axis=-1, keepdims=True)


def _tile_offsets(rows, width, tq):
    row = lax.broadcasted_iota(jnp.int32, (rows, width), 0)
    col = lax.broadcasted_iota(jnp.int32, (rows, width), 1)
    return (row & (tq - 1)) - col


ATTN_ROW_GROUP = 128


def _causal_softmax_pv(qs, kbf, vbf, bias, width, tq, scale=None, block_off=None, lam=None):
    rg = min(ATTN_ROW_GROUP, tq)
    assert tq % rg == 0
    groups = [slice(i * rg, (i + 1) * rg) for i in range(tq // rg)]
    units = [(g, q) for g in groups for q in qs]
    past = width - tq
    ss = [_dot_nt(q[g], kbf[0:width, :]) for g, q in units]
    if scale is not None:
        ss = [s * scale for s in ss]
    ss = [s + bias[g, 0:width] for s, (g, _) in zip(ss, units)]
    masked = []
    for s, (g, _) in zip(ss, units):
        pieces = []
        for j in range(past // tq):
            piece = s[:, j * tq:(j + 1) * tq]
            pieces.append(piece if block_off is None else piece + block_off[g, j:j + 1])
        own = jnp.where(_tile_offsets(rg, tq, tq) + g.start >= 0, s[:, past:], NEG)
        masked.append(jnp.concatenate(pieces + [own], axis=1) if pieces else own)
    ps = [jnp.exp(s - jnp.max(s, axis=-1, keepdims=True)) for s in masked]
    ps = [p * (1.0 / jnp.sum(p, axis=-1, keepdims=True)) for p in ps]
    if len(qs) == 2:
        ps = [ps[2 * i] - lam * ps[2 * i + 1] for i in range(len(groups))]
    os = [_dot(p.astype(BF16), vbf[0:width, :]) for p in ps]
    return jnp.concatenate(os, axis=0) if len(os) > 1 else os[0]


def _diff_prompt_kernel(q_ref, kbf, vbf, dl_ref, gain_ref, slope_ref, o_ref, bias, *, tq, nq, lam_init):
    qi = pl.program_id(2)

    @pl.when(qi == 0)
    def _():
        bias[...] = -slope_ref[:, 0:1] * _tile_offsets(tq, bias.shape[1], tq).astype(F32)

    q2 = _split_halves(q_ref[...] * (DA_HALF ** -0.5))
    lam = _diff_lambda(dl_ref[...], lam_init)

    for v in range(nq):
        @pl.when(qi == v)
        def _(v=v):
            out = _causal_softmax_pv([q2[:tq], q2[tq:]], kbf, vbf, bias, (v + 1) * tq, tq, lam=lam)
            o_ref[...] = (_rms(out, gain_ref[...]) * (1.0 - lam_init)).astype(o_ref.dtype)


def _diff_attn_prompt(q, kv, dl, gain_row, slopes, lam_init, b, t):
    n = b * t
    tq = ATTN_TQ
    assert t % tq == 0 and tq & (tq - 1) == 0
    nq = t // tq
    assert nq <= 16
    kern = functools.partial(_diff_prompt_kernel, tq=tq, nq=nq, lam_init=lam_init)
    return pl.pallas_call(
        kern,
        grid=(b, HA, nq),
        in_specs=[
            pl.BlockSpec((tq, DA), lambda bi, h, qi: (bi * nq + qi, h)),
            pl.BlockSpec((t, DA), lambda bi, h, qi: (bi, h)),
            pl.BlockSpec((t, DA), lambda bi, h, qi: (bi, HA + h)),
            pl.BlockSpec((4, DA_HALF), lambda bi, h, qi: (0, 0)),
            pl.BlockSpec((1, DA), lambda bi, h, qi: (0, 0)),
            pl.BlockSpec((None, 1, LANES), lambda bi, h, qi: (h, 0, 0)),
        ],
        out_specs=pl.BlockSpec((tq, DA), lambda bi, h, qi: (bi * nq + qi, h)),
        out_shape=jax.ShapeDtypeStruct((n, HA * DA), BF16),
        scratch_shapes=[pltpu.VMEM((tq, t), F32)],
        compiler_params=_cparams(("parallel", "parallel", "arbitrary")),
        name="diff_attn_prompt",
    )(q, kv, kv, dl, gain_row, slopes)


def _moba_prompt_kernel(q_ref, kbf, vbf, means_ref, slope_ref, o_ref, means, bias, *, tq, nb):
    qi = pl.program_id(2)
    slope = slope_ref[:, 0:1]

    @pl.when(qi == 0)
    def _():
        bias[...] = -slope * _tile_offsets(tq, bias.shape[1], tq).astype(F32)
        means[...] = jnp.zeros(means.shape, F32)
        means[0:nb, :] = means_ref[...]

    q = q_ref[...]
    scale = DB ** -0.5
    nbp = -(-nb // SUBLANES) * SUBLANES
    gate_t = _dot_nt(means[...].astype(BF16), q)[0:nbp]
    blk = lax.broadcasted_iota(jnp.int32, gate_t.shape, 0)
    sel_t = _top_blocks(gate_t, blk < qi, axis=0)
    sel = jnp.concatenate([sel_t, jnp.zeros((LANES - nbp, tq), F32)], axis=0).T
    off = jnp.where(sel > 0.5, 0.0, NEG)

    for v in range(nb):
        @pl.when(qi == v)
        def _(v=v):
            o_ref[...] = _causal_softmax_pv([q], kbf, vbf, bias, (v + 1) * tq, tq, scale, off).astype(o_ref.dtype)


def _moba_attn_prompt(q, kv, means, slopes, b, t):
    n = b * t
    tq = ATTN_TQ
    assert tq == MOBA_BLOCK and t % tq == 0 and tq & (tq - 1) == 0
    nq = t // tq
    assert nq <= 16
    kern = functools.partial(_moba_prompt_kernel, tq=tq, nb=nq)
    return pl.pallas_call(
        kern,
        grid=(b, HB, nq),
        in_specs=[
            pl.BlockSpec((tq, DB), lambda bi, h, qi: (bi * nq + qi, h)),
            pl.BlockSpec((t, DB), lambda bi, h, qi: (bi, h)),
            pl.BlockSpec((t, DB), lambda bi, h, qi: (bi, HB + h)),
            pl.BlockSpec((None, nq, DB), lambda bi, h, qi: (bi, 0, h)),
            pl.BlockSpec((None, 1, LANES), lambda bi, h, qi: (h, 0, 0)),
        ],
        out_specs=pl.BlockSpec((tq, DB), lambda bi, h, qi: (bi * nq + qi, h)),
        out_shape=jax.ShapeDtypeStruct((n, HB * DB), BF16),
        scratch_shapes=[pltpu.VMEM((LANES, DB), F32), pltpu.VMEM((tq, t), F32)],
        compiler_params=_cparams(("parallel", "parallel", "arbitrary")),
        name="moba_attn_prompt",
    )(q, kv, kv, means, slopes)


def _head_rows(pages, kv, h, n_heads, page):
    parts = [pg[pl.ds(kv * n_heads + h, page, stride=2 * n_heads), :] for pg in pages]
    return jnp.concatenate(parts, axis=0).astype(BF16)


def _diff_sample_kernel(pt_ref, q_ref, kvn_ref, dl_ref, gain_ref, *rest, n_in, ts, page, past_len, lam_init,
                        slopes):
    del pt_ref
    pages = rest[:n_in]
    o_ref = rest[n_in]
    s_sc, v_sc = rest[n_in + 1:]
    i = pl.program_id(1)
    rows = 2 * ts
    nkeys = n_in * page
    row1 = lax.broadcasted_iota(jnp.int32, (rows, 1), 0)
    tok1 = jnp.where(row1 < ts, row1, row1 - ts)
    lane = lax.broadcasted_iota(jnp.int32, (rows, nkeys), 1)
    dist = ((past_len + tok1) - (i * nkeys + lane)).astype(F32)
    q_all = q_ref[...]
    q2 = [_split_halves(q_all[:, h * DA:(h + 1) * DA] * (DA_HALF ** -0.5)).astype(BF16) for h in range(HA)]

    off = pl.multiple_of(i * nkeys, nkeys)
    heads = range(HA)
    ss = [_dot_nt(q2[h], _head_rows(pages, 0, h, HA, page)) - slopes[h] * dist for h in heads]
    for h in heads:
        s_sc[h, :, pl.ds(off, nkeys)] = ss[h]
        v_sc[h, pl.ds(off, nkeys), :] = _head_rows(pages, 1, h, HA, page)

    @pl.when(i == pl.num_programs(1) - 1)
    def _():
        lam = _diff_lambda(dl_ref[...], lam_init)
        outs = _sample_softmax_pv(s_sc, v_sc, [q2[h].astype(F32) for h in heads], kvn_ref, HA, tok1, slopes, 1.0,
                                  ts, lam)
        for h in heads:
            o_ref[:, h * DA:(h + 1) * DA] = _rms(outs[h], gain_ref[...]) * (1.0 - lam_init)


def _page_specs(layer, n_pages, n_in, rows):
    def spec(j):
        return pl.BlockSpec((None, None, rows, LANES),
                            lambda b, i, pt: (layer, pt[b * n_pages + i * n_in + j], 0, 0))
    return [spec(j) for j in range(n_in)]


def _diff_attn_sample(q3, kvn3, cache4, pt_flat, dl, gain_row, slopes, lam_init, layer, n_pages, page):
    bs, ts, _ = q3.shape
    assert 2 * ts == SUBLANES
    n_in = min(PAGES_PER_STEP, n_pages)
    assert n_pages % n_in == 0
    rows = cache4.shape[2]
    kern = functools.partial(_diff_sample_kernel, n_in=n_in, ts=ts, page=page, past_len=n_pages * page,
                             lam_init=lam_init, slopes=slopes)
    grid_spec = pltpu.PrefetchScalarGridSpec(
        num_scalar_prefetch=1,
        grid=(bs, n_pages // n_in),
        in_specs=[
            pl.BlockSpec((None, ts, HA * DA), lambda b, i, pt: (b, 0, 0)),
            pl.BlockSpec((None, ts, 2 * HA * DA), lambda b, i, pt: (b, 0, 0)),
            pl.BlockSpec((4, DA_HALF), lambda b, i, pt: (0, 0)),
            pl.BlockSpec((1, DA), lambda b, i, pt: (0, 0)),
        ] + _page_specs(layer, n_pages, n_in, rows),
        out_specs=pl.BlockSpec((None, ts, HA * DA), lambda b, i, pt: (b, 0, 0)),
        scratch_shapes=[pltpu.VMEM((HA, 2 * ts, n_pages * page), F32), pltpu.VMEM((HA, n_pages * page, DA), BF16)],
    )
    return pl.pallas_call(
        kern,
        grid_spec=grid_spec,
        out_shape=jax.ShapeDtypeStruct((bs, ts, HA * DA), F32),
        compiler_params=_cparams(("parallel", "arbitrary")),
        name="diff_attn_sample",
    )(pt_flat, q3, kvn3, dl, gain_row, *([cache4] * n_in))


def _moba_sample_kernel(pt_ref, q_ref, kvn_ref, *rest, n_in, ts, page, past_len, slopes):
    del pt_ref
    pages = rest[:n_in]
    o_ref = rest[n_in]
    means_sc, s_sc, v_sc = rest[n_in + 1:]
    i = pl.program_id(1)
    rows = SUBLANES
    nkeys = n_in * page
    row1 = lax.broadcasted_iota(jnp.int32, (rows, 1), 0)
    tok1 = jnp.where(row1 < ts, row1, row1 - ts)
    lane = lax.broadcasted_iota(jnp.int32, (rows, nkeys), 1)
    dist = ((past_len + tok1) - (i * nkeys + lane)).astype(F32)
    scale = DB ** -0.5
    ppb = MOBA_BLOCK // page
    blocks_per_step = n_in // ppb
    nb_past = past_len // MOBA_BLOCK
    q_all = q_ref[...]
    q8 = [jnp.concatenate([q_all[:, h * DB:(h + 1) * DB]] * 2, axis=0).astype(BF16) for h in range(HB)]

    @pl.when(i == 0)
    def _():
        means_sc[...] = jnp.zeros(means_sc.shape, F32)

    off = pl.multiple_of(i * nkeys, nkeys)
    boff = pl.multiple_of(i * blocks_per_step, blocks_per_step)
    heads = range(HB)
    kparts = [[pg[pl.ds(h, page, stride=2 * HB), :] for pg in pages] for h in heads]
    ss = [_dot_nt(q8[h], jnp.concatenate(kparts[h], axis=0).astype(BF16)) * scale - slopes[h] * dist for h in heads]
    for h in heads:
        s_sc[h, :, pl.ds(off, nkeys)] = ss[h]
        v_sc[h, pl.ds(off, nkeys), :] = _head_rows(pages, 1, h, HB, page)
        sums = [jnp.sum(functools.reduce(jnp.add, kparts[h][jb * ppb:(jb + 1) * ppb]), axis=0, keepdims=True)
                for jb in range(blocks_per_step)]
        means_sc[h, pl.ds(boff, blocks_per_step), :] = jnp.concatenate(sums, axis=0) * (1.0 / MOBA_BLOCK)

    @pl.when(i == pl.num_programs(1) - 1)
    def _():
        bias = []
        for h in heads:
            gate = _dot_nt(q8[h], means_sc[h].astype(BF16))
            glane = lax.broadcasted_iota(jnp.int32, gate.shape, 1)
            block_off = jnp.where(_top_blocks(gate, glane < nb_past) > 0.5, 0.0, NEG)
            bias.append(jnp.concatenate([jnp.broadcast_to(block_off[:, j:j + 1], (rows, MOBA_BLOCK))
                                         for j in range(nb_past)], axis=1))
        outs = _sample_softmax_pv(s_sc, v_sc, [q8[h].astype(F32) for h in heads], kvn_ref, HB, tok1, slopes, scale,
                                  ts, bias=bias)
        for h in heads:
            o_ref[:, h * DB:(h + 1) * DB] = outs[h]


def _moba_attn_sample(q3, kvn3, cache4, pt_flat, slopes, layer, n_pages, page):
    bs, ts, _ = q3.shape
    assert 2 * ts == SUBLANES
    past_len = n_pages * page
    assert past_len % MOBA_BLOCK == 0 and ts <= MOBA_BLOCK and past_len // MOBA_BLOCK <= LANES
    n_in = min(PAGES_PER_STEP, n_pages)
    assert n_pages % n_in == 0 and MOBA_BLOCK % page == 0 and (n_in * page) % MOBA_BLOCK == 0
    assert n_in == n_pages or (n_in * page // MOBA_BLOCK) % SUBLANES == 0
    rows = cache4.shape[2]
    kern = functools.partial(_moba_sample_kernel, n_in=n_in, ts=ts, page=page, past_len=past_len, slopes=slopes)
    grid_spec = pltpu.PrefetchScalarGridSpec(
        num_scalar_prefetch=1,
        grid=(bs, n_pages // n_in),
        in_specs=[
            pl.BlockSpec((None, ts, HB * DB), lambda b, i, pt: (b, 0, 0)),
            pl.BlockSpec((None, ts, 2 * HB * DB), lambda b, i, pt: (b, 0, 0)),
        ] + _page_specs(layer, n_pages, n_in, rows),
        out_specs=pl.BlockSpec((None, ts, HB * DB), lambda b, i, pt: (b, 0, 0)),
        scratch_shapes=[pltpu.VMEM((HB, LANES, DB), F32), pltpu.VMEM((HB, SUBLANES, past_len), F32),
                        pltpu.VMEM((HB, past_len, DB), BF16)],
    )
    return pl.pallas_call(
        kern,
        grid_spec=grid_spec,
        out_shape=jax.ShapeDtypeStruct((bs, ts, HB * DB), F32),
        compiler_params=_cparams(("parallel", "arbitrary")),
        name="moba_attn_sample",
    )(pt_flat, q3, kvn3, *([cache4] * n_in))


def _split_bf16(x, parts):
    out = []
    for _ in range(parts):
        hi = x.astype(BF16)
        out.append(hi)
        x = x - hi.astype(F32)
    return out


def _dot3(a, b):
    a_hi, a_lo = _split_bf16(a, 2)
    b_hi, b_lo = _split_bf16(b, 2)
    return _dot(a_hi, b_hi) + (_dot(a_hi, b_lo) + _dot(a_lo, b_hi))


def _unit_lower_inverses(nmats):
    c = nmats[0].shape[0]
    row = lax.broadcasted_iota(jnp.int32, (c, c), 0)
    col = lax.broadcasted_iota(jnp.int32, (c, c), 1)
    eye = (row == col).astype(F32)
    shift = 3
    same = lax.shift_right_logical(row, shift) == lax.shift_right_logical(col, shift)
    n0 = [jnp.where(same, n, 0.0) for n in nmats]
    n2 = [_dot3(a, a) for a in n0]
    n4 = [_dot3(a, a) for a in n2]
    inv = [_dot3(eye - a, eye + b) for a, b in zip(n0, n2)]
    inv = [_dot3(a, eye + b) for a, b in zip(inv, n4)]
    while (1 << shift) < c:
        shift += 1
        same2 = lax.shift_right_logical(row, shift) == lax.shift_right_logical(col, shift)
        grow = same2 & jnp.logical_not(same)
        left = [_dot3(a, jnp.where(grow, n, 0.0)) for a, n in zip(inv, nmats)]
        inv = [a - _dot3(b, a) for a, b in zip(inv, left)]
        same = same2
    return inv


def _gdn_prep_kernel(cqkv_ref, halo_ref, ba_ref, prev_ref, cw_ref, alog_ref, dtb_ref,
                     u_ref, w_ref, qd_ref, kd_ref, qk_ref, egl_ref, xbuf, *, chunk, n_chunks, t_valid):
    c = pl.program_id(1)
    rows = n_chunks * chunk
    halo = SUBLANES
    xbuf[0:halo, :] = jnp.where(c == 0, prev_ref[...], halo_ref[...])
    xbuf[halo:halo + rows, :] = cqkv_ref[...]
    conv = cw_ref[0:1, :] * xbuf[halo - 3:halo - 3 + rows, :]
    for j in range(1, CONV_W):
        conv = conv + cw_ref[j:j + 1, :] * xbuf[halo - 3 + j:halo - 3 + j + rows, :]
    conv = conv * jax.nn.sigmoid(conv)

    rowc = lax.broadcasted_iota(jnp.int32, (rows, 1), 0)
    valid = (c * rows + rowc) < t_valid
    ba = ba_ref[...]
    beta_all = jnp.where(valid, jax.nn.sigmoid(ba), 0.0)
    g_all = jnp.where(valid, -jnp.exp(alog_ref[...]) * jax.nn.softplus(ba + dtb_ref[...]), 0.0)
    ri = lax.broadcasted_iota(jnp.int32, (chunk, chunk), 0)
    ci = lax.broadcasted_iota(jnp.int32, (chunk, chunk), 1)
    tril16 = (ri >= ci).astype(BF16)
    eye16 = (ri == ci).astype(BF16)

    probs = []
    for g in range(n_chunks):
        r0 = g * chunk
        gc_all = functools.reduce(jnp.add, [_dot(tril16, p) for p in _split_bf16(g_all[r0:r0 + chunk], 3)])
        gc_t = functools.reduce(jnp.add, [_dot_tn(p, eye16) for p in _split_bf16(gc_all, 3)])
        egl_ref[r0:r0 + chunk, :] = jnp.broadcast_to(jnp.exp(gc_all[chunk - 1:chunk, :]), (chunk, LANES))
        for h in range(HC):
            qh = conv[r0:r0 + chunk, h * DK:(h + 1) * DK]
            kh = conv[r0:r0 + chunk, (HC + h) * DK:(HC + h + 1) * DK]
            vh = conv[r0:r0 + chunk, 2 * HC * DK + h * DV:2 * HC * DK + (h + 1) * DV]
            qh = qh * lax.rsqrt(jnp.sum(qh * qh, axis=-1, keepdims=True) + 1e-6) * (DK ** -0.5)
            kh = kh * lax.rsqrt(jnp.sum(kh * kh, axis=-1, keepdims=True) + 1e-6)
            beta = _lane_column(beta_all[r0:r0 + chunk], h)
            gc = _lane_column(gc_all, HC + h)
            gc_row = gc_t[HC + h:HC + h + 1, :]
            probs.append(dict(r0=r0, h=h, qh=qh, kh=kh, vh=vh, beta=beta, gc=gc, diff=gc - gc_row,
                              gc_last=gc_row[:, chunk - 1:chunk], kbeta=kh * beta, k16=kh.astype(BF16)))
    nmats = [_dot_nt(p["kbeta"].astype(BF16), p["k16"]) * jnp.exp(jnp.where(ri > ci, p["diff"], NEG)) for p in probs]
    qks = [_dot_nt(p["qh"].astype(BF16), p["k16"]) * jnp.exp(jnp.where(ri >= ci, p["diff"], NEG)) for p in probs]
    invs = _unit_lower_inverses(nmats)
    uws = []
    for p, inv in zip(probs, invs):
        egc = jnp.exp(p["gc"])
        rhs = jnp.concatenate([(p["vh"] * p["beta"]).astype(BF16), (p["kbeta"] * egc).astype(BF16)], axis=1)
        uws.append(_dot(inv.astype(BF16), rhs))
        p["egc"] = egc
    for p, uw, qk in zip(probs, uws, qks):
        rs = slice(p["r0"], p["r0"] + chunk)
        hs = slice(p["h"] * DK, (p["h"] + 1) * DK)
        u_ref[rs, hs] = uw[:, :DV]
        w_ref[rs, hs] = uw[:, DV:].astype(BF16)
        qk_ref[rs, p["h"] * chunk:(p["h"] + 1) * chunk] = qk.astype(BF16)
        qd_ref[rs, hs] = (p["qh"] * p["egc"]).astype(BF16)
        kd_ref[rs, hs] = (p["kh"] * jnp.exp(p["gc_last"] - p["gc"])).astype(BF16)


def _gdn_scan_kernel(u_ref, w_ref, qd_ref, kd_ref, qk_ref, egl_ref, cg_ref, s0_ref, gain_ref, o_ref, s_ref,
                     *, n_seq, chunk):
    c = pl.program_id(1)

    @pl.when(c == 0)
    def _():
        s_ref[...] = s0_ref[...]

    chains = [(bb, h, slice(h * DK, (h + 1) * DK)) for bb in range(n_seq) for h in range(HC)]
    s_old = [s_ref[bb, h] for bb, h, _ in chains]
    s16 = [s.astype(BF16) for s in s_old]
    ws = [_dot(jnp.concatenate([w_ref[bb, :, hs], qd_ref[bb, :, hs]], axis=0), s)
          for (bb, _, hs), s in zip(chains, s16)]
    v16 = [(u_ref[bb, :, hs] - x[:chunk]).astype(BF16) for (bb, _, hs), x in zip(chains, ws)]
    o = [x[chunk:] + _dot(qk_ref[bb, :, h * chunk:(h + 1) * chunk], v) for (bb, h, _), x, v in zip(chains, ws, v16)]
    kv = [_dot_tn(kd_ref[bb, :, hs], v) for (bb, _, hs), v in zip(chains, v16)]
    for (bb, h, hs), s, upd, out in zip(chains, s_old, kv, o):
        decay = _lane_column(egl_ref[bb, 0:1, :], HC + h)
        s_ref[bb, h] = s * decay + upd
        gate = cg_ref[bb, :, hs]
        o_ref[bb, :, hs] = (_rms(out, gain_ref[...]) * (gate * jax.nn.sigmoid(gate))).astype(o_ref.dtype)


def _gdn(cqkv, cg, ba, prev8, s0, conv_w8, alog_row, dtb_row, gain_row, b, t, t_valid):
    chunk = GDN_CHUNK
    assert t % chunk == 0 and DK == DV
    nc = t // chunk
    n = b * t
    g = 2 if nc % 2 == 0 else 1
    rows = g * chunk
    steps = nc // g
    hb = SUBLANES
    prep = functools.partial(_gdn_prep_kernel, chunk=chunk, n_chunks=g, t_valid=t_valid)
    blk = lambda width: pl.BlockSpec((rows, width), lambda bi, c: (bi * steps + c, 0))
    u, w, qd, kd, qk, egl = pl.pallas_call(
        prep,
        grid=(b, steps),
        in_specs=[
            blk(C_CONV),
            pl.BlockSpec((hb, C_CONV), lambda bi, c: (jnp.maximum((bi * t + c * rows) // hb - 1, 0), 0)),
            blk(LANES),
            pl.BlockSpec((None, hb, C_CONV), lambda bi, c: (bi, 0, 0)),
            pl.BlockSpec((SUBLANES, C_CONV), lambda bi, c: (0, 0)),
            pl.BlockSpec((1, LANES), lambda bi, c: (0, 0)),
            pl.BlockSpec((1, LANES), lambda bi, c: (0, 0)),
        ],
        out_specs=[blk(HC * DV), blk(HC * DK), blk(HC * DK), blk(HC * DK), blk(HC * chunk), blk(LANES)],
        out_shape=[jax.ShapeDtypeStruct((n, HC * DV), F32), jax.ShapeDtypeStruct((n, HC * DK), BF16),
                   jax.ShapeDtypeStruct((n, HC * DK), BF16), jax.ShapeDtypeStruct((n, HC * DK), BF16),
                   jax.ShapeDtypeStruct((n, HC * chunk), BF16), jax.ShapeDtypeStruct((n, LANES), F32)],
        scratch_shapes=[pltpu.VMEM((rows + hb, C_CONV), F32)],
        compiler_params=_cparams(("parallel", "parallel")),
        name="gdn_prep",
    )(cqkv, cqkv, ba, prev8, conv_w8, alog_row, dtb_row)

    n_seq = 4 if b % 4 == 0 else (2 if b % 2 == 0 else 1)
    scan = functools.partial(_gdn_scan_kernel, n_seq=n_seq, chunk=chunk)
    seq = lambda width: pl.BlockSpec((n_seq, chunk, width), lambda bg, c: (bg, c, 0))
    state = pl.BlockSpec((n_seq, HC, DK, DV), lambda bg, c: (bg, 0, 0, 0))
    as3 = lambda a: a.reshape(b, t, a.shape[-1])
    out, s_new = pl.pallas_call(
        scan,
        grid=(b // n_seq, nc),
        in_specs=[seq(HC * DV), seq(HC * DK), seq(HC * DK), seq(HC * DK), seq(HC * chunk), seq(LANES),
                  seq(HC * DV), state, pl.BlockSpec((1, DV), lambda bg, c: (0, 0))],
        out_specs=[seq(HC * DV), state],
        out_shape=[jax.ShapeDtypeStruct((b, t, HC * DV), BF16), jax.ShapeDtypeStruct((b, HC, DK, DV), F32)],
        compiler_params=_cparams(("parallel", "arbitrary")),
        name="gdn_scan",
    )(as3(u), as3(w), as3(qd), as3(kd), as3(qk), as3(egl), as3(cg), s0, gain_row)
    return out.reshape(n, HC * DV), s_new


def _route(logits):
    lane = lax.broadcasted_iota(jnp.int32, logits.shape, 1).astype(F32)

    def argmax_first(x):
        mx = jnp.max(x, axis=-1, keepdims=True)
        return mx, jnp.min(jnp.where(x == mx, lane, float(LANES)), axis=-1, keepdims=True)

    lg = jnp.where(lane < N_GROUPS, logits, NEG)
    mg, g_top = argmax_first(lg)
    pg_top = 1.0 / jnp.sum(jnp.exp(lg - mg), axis=-1, keepdims=True)
    lo = N_GROUPS + g_top * EXPERTS_PER_GROUP
    le = jnp.where((lane >= lo) & (lane < lo + EXPERTS_PER_GROUP), logits, NEG)
    m1, i1 = argmax_first(le)
    m2, i2 = argmax_first(jnp.where(lane == i1, NEG, le))
    se = jnp.sum(jnp.exp(le - m1), axis=-1, keepdims=True)
    pe1 = 1.0 / se
    pe2 = jnp.exp(m2 - m1) / se
    w1 = pg_top * pe1 / (pe1 + pe2)
    w2 = pg_top * pe2 / (pe1 + pe2)
    out = jnp.where(lane == 0, i1 - N_GROUPS, 0.0)
    out = jnp.where(lane == 1, i2 - N_GROUPS, out)
    out = jnp.where(lane == 2, w1, out)
    chosen = jnp.where((lane == i1) | (lane == i2), 1.0, 0.0)
    return jnp.where(lane == 3, w2, out), jnp.sum(chosen, axis=0, keepdims=True)


def _merge_kernel(x_ref, a_ref, b_ref, c_ref, gm_ref, wg_ref, wb_ref, wo_ref, gf_ref, wr_ref,
                  x1_ref, h2_ref, route_ref, counts_ref):
    x = x_ref[...]
    d = x.shape[-1]
    h = _rms(x, gm_ref[...]).astype(BF16)
    mix = None
    for n, br in enumerate((a_ref, b_ref, c_ref)):
        term = jax.nn.sigmoid(_dot(h, wg_ref[:, n * d:(n + 1) * d])) * _dot(br[...], wb_ref[n])
        mix = term if mix is None else mix + term
    x1 = x + _dot(mix.astype(BF16), wo_ref[...])
    x1_ref[...] = x1
    h2 = _rms(x1, gf_ref[...]).astype(BF16)
    h2_ref[...] = h2
    route, counts = _route(_dot(h2, wr_ref[...]))
    route_ref[...] = route
    counts_ref[...] = jnp.broadcast_to(counts, counts_ref.shape)


def _merge(x2d, out_a, out_b, out_c, gm_row, w_gate, w_branch, w_out, gf_row, w_router):
    n, d = x2d.shape
    tm = _row_tile(n, ROW_TILE)
    row = lambda i: (i, 0)
    return pl.pallas_call(
        _merge_kernel,
        grid=(n // tm,),
        in_specs=[
            pl.BlockSpec((tm, d), row), pl.BlockSpec((tm, BRANCH_W), row), pl.BlockSpec((tm, BRANCH_W), row),
            pl.BlockSpec((tm, BRANCH_W), row), _resident((1, d)), _resident(w_gate.shape),
            _resident(w_branch.shape), _resident(w_out.shape), _resident((1, d)), _resident(w_router.shape),
        ],
        out_specs=[pl.BlockSpec((tm, d), row), pl.BlockSpec((tm, d), row), pl.BlockSpec((tm, LANES), row),
                   pl.BlockSpec((None, SUBLANES, LANES), lambda i: (i, 0, 0))],
        out_shape=[jax.ShapeDtypeStruct((n, d), F32), jax.ShapeDtypeStruct((n, d), BF16),
                   jax.ShapeDtypeStruct((n, LANES), F32), jax.ShapeDtypeStruct((n // tm, SUBLANES, LANES), F32)],
        compiler_params=_cparams(("parallel",)),
        name="merge",
    )(x2d, out_a, out_b, out_c, gm_row, w_gate, w_branch, w_out, gf_row, w_router)


def _moe_kernel(te_ref, xs_ref, rw_ref, wg_ref, wu_ref, wd_ref, y_ref):
    del te_ref
    x = xs_ref[...]
    g = _dot(x, wg_ref[...].astype(BF16))
    a = (g * jax.nn.sigmoid(g)) * _dot(x, wu_ref[...].astype(BF16)) * rw_ref[...]
    y_ref[...] = _dot(a.astype(BF16), wd_ref[...].astype(BF16))


def _moe_experts(xs, row_w, tile_expert, w_g, w_u, w_d, layer):
    m, d = xs.shape
    f = w_g.shape[-1]
    grid_spec = pltpu.PrefetchScalarGridSpec(
        num_scalar_prefetch=1,
        grid=(m // MOE_TILE,),
        in_specs=[
            pl.BlockSpec((MOE_TILE, d), lambda i, te: (i, 0)),
            pl.BlockSpec((MOE_TILE, 1), lambda i, te: (i, 0)),
            pl.BlockSpec((None, None, d, f), lambda i, te: (layer, te[i], 0, 0)),
            pl.BlockSpec((None, None, d, f), lambda i, te: (layer, te[i], 0, 0)),
            pl.BlockSpec((None, None, f, d), lambda i, te: (layer, te[i], 0, 0)),
        ],
        out_specs=pl.BlockSpec((MOE_TILE, d), lambda i, te: (i, 0)),
    )
    return pl.pallas_call(
        _moe_kernel,
        grid_spec=grid_spec,
        out_shape=jax.ShapeDtypeStruct((m, d), F32),
        compiler_params=_cparams(("arbitrary",)),
        name="moe_experts",
    )(tile_expert, xs, row_w, w_g, w_u, w_d)


def _combine_kernel(x_ref, y1_ref, y2_ref, gain_ref, o_ref, *, final_norm):
    x = x_ref[...] + (y1_ref[...] + y2_ref[...])
    o_ref[...] = _rms(x, gain_ref[...]) if final_norm else x


def _combine(x1, y1, y2, gain_row, final_norm):
    n, d = x1.shape
    tm = _row_tile(n, ROW_TILE)
    row = lambda i: (i, 0)
    return pl.pallas_call(
        functools.partial(_combine_kernel, final_norm=final_norm),
        grid=(n // tm,),
        in_specs=[pl.BlockSpec((tm, d), row), pl.BlockSpec((tm, d), row), pl.BlockSpec((tm, d), row),
                  _resident((1, d))],
        out_specs=pl.BlockSpec((tm, d), row),
        out_shape=jax.ShapeDtypeStruct((n, d), F32),
        compiler_params=_cparams(("parallel",)),
        name="moe_combine",
    )(x1, y1, y2, gain_row)


def _dispatch_tables(route, counts, tile):
    n = route.shape[0]
    eid = route[:, 0:2].astype(jnp.int32).reshape(-1)
    wsel = route[:, 2:4].reshape(-1)
    n_assign = 2 * n
    m_pad = -(-(n_assign + N_EXPERTS * (tile - 1)) // tile) * tile
    n_tiles = m_pad // tile
    order = jnp.argsort(eid, stable=True).astype(jnp.int32)
    padded = (counts + tile - 1) // tile * tile
    ends_p = jnp.cumsum(padded)
    starts_p = ends_p - padded
    starts = jnp.cumsum(counts) - counts
    tile_start = jnp.arange(n_tiles, dtype=jnp.int32) * tile
    tile_expert = jnp.sum((ends_p[None, :] <= tile_start[:, None]).astype(jnp.int32), axis=1)
    tile_expert = jnp.minimum(tile_expert, N_EXPERTS - 1)
    base = starts[tile_expert] - starts_p[tile_expert] + tile_start
    limit = starts[tile_expert] + counts[tile_expert]
    sorted_pos = base[:, None] + jnp.arange(tile, dtype=jnp.int32)[None, :]
    valid = (sorted_pos < limit[:, None]).reshape(-1)
    assign = order[jnp.clip(sorted_pos.reshape(-1), 0, n_assign - 1)]
    row_src = jnp.where(valid, assign // 2, 0)
    row_w = jnp.where(valid, wsel[assign], 0.0)
    key = jnp.where(valid, assign, n_assign + jnp.arange(m_pad, dtype=jnp.int32))
    slot_row = jnp.argsort(key)[:n_assign].astype(jnp.int32).reshape(n, 2)
    return row_src, row_w.reshape(m_pad, 1), slot_row, tile_expert


def _moe_dispatch(h2, route, tile_counts):
    counts = jnp.sum(tile_counts[:, 0, N_GROUPS:N_GROUPS + N_EXPERTS], axis=0).astype(jnp.int32)
    row_src, row_w, slot_row, tile_expert = _dispatch_tables(route, counts, MOE_TILE)
    return h2[row_src], row_w, slot_row, tile_expert


def _moe_finish(x1, dispatched, w_g, w_u, w_d, layer, gain_row, final_norm):
    xs, row_w, slot_row, tile_expert = dispatched
    ys = _moe_experts(xs, row_w, tile_expert, w_g, w_u, w_d, layer)
    return _combine(x1, ys[slot_row[:, 0]], ys[slot_row[:, 1]], gain_row, final_norm)


def _lane_row(vec, offset):
    return jnp.zeros((1, LANES), F32).at[0, offset:offset + vec.shape[0]].set(vec.astype(F32))


def _layer_weights(l, w_in, norm_mixer, diff_lambda, diff_norm_gain, gdn_conv_w, gdn_a_log, gdn_dt_bias,
                   gdn_norm_gain, w_branch, w_out, norm_ffn, router_group, router_expert, expert_w_gate,
                   expert_w_up, expert_w_down):
    d = w_in.shape[1]
    w_mix = jnp.pad(w_in[l, :, :N_MIX_COLS], ((0, 0), (0, N_MIX_PAD - N_MIX_COLS))).astype(BF16)
    w_router = jnp.concatenate([router_group[l], router_expert[l]], axis=1)
    w_router = jnp.pad(w_router, ((0, 0), (0, LANES - w_router.shape[1]))).astype(BF16)
    return dict(
        gm=norm_mixer[l].reshape(1, d), w_mix=w_mix, w_gate=w_in[l, :, N_MIX_COLS:].astype(BF16),
        dl=diff_lambda[l].astype(F32), dgain=diff_norm_gain[l].reshape(1, DA).astype(F32),
        conv_w=jnp.pad(gdn_conv_w[l].astype(F32), ((0, SUBLANES - CONV_W), (0, 0))),
        alog=_lane_row(gdn_a_log[l], HC), dtb=_lane_row(gdn_dt_bias[l], HC),
        ggain=gdn_norm_gain[l].reshape(1, DV).astype(F32),
        w_branch=w_branch[l].astype(BF16), w_out=w_out[l].astype(BF16), gf=norm_ffn[l].reshape(1, d),
        w_router=w_router, w_g=expert_w_gate, w_u=expert_w_up, w_d=expert_w_down, layer=l,
    )


def _slope_rows(slopes):
    return jnp.broadcast_to(jnp.asarray(slopes, F32)[:, None, None], (len(slopes), 1, LANES))


def _prompt_layer(x2d, lw, lam_init, b, t):
    slopes_a, slopes_b = _alibi_slopes()
    q_a, kv_a, kva16, q_b, kv_b, kvb16, means, cqkv, cg, ba = _inproj_prompt(x2d, lw["gm"], lw["w_mix"])
    out_a = _diff_attn_prompt(q_a, kva16, lw["dl"], lw["dgain"], _slope_rows(slopes_a), lam_init, b, t)
    out_b = _moba_attn_prompt(q_b, kvb16, means.reshape(b, t // MOBA_BLOCK, HB * DB), _slope_rows(slopes_b), b, t)
    prev8 = jnp.zeros((b, SUBLANES, C_CONV), F32)
    s0 = jnp.zeros((b, HC, DK, DV), F32)
    out_c, s_new = _gdn(cqkv, cg, ba, prev8, s0, lw["conv_w"], lw["alog"], lw["dtb"], lw["ggain"], b, t, t)
    x1, h2, route, counts = _merge(x2d, out_a, out_b, out_c, lw["gm"], lw["w_gate"], lw["w_branch"], lw["w_out"],
                                   lw["gf"], lw["w_router"])
    conv_new = cqkv.reshape(b, t, C_CONV)[:, t - (CONV_W - 1):]
    return (x1, _moe_dispatch(h2, route, counts)), kv_a, kv_b, s_new, conv_new


def _layer_finish(pending, lw, final_gain):
    x1, dispatched = pending
    gain = lw["gf"] if final_gain is None else final_gain
    return _moe_finish(x1, dispatched, lw["w_g"], lw["w_u"], lw["w_d"], lw["layer"], gain, final_gain is not None)


def _sample_layer(x2d, lw, lam_init, bs, ts, layer, cache_a, cache_b, pt_flat, n_pages, page, conv_prev, s_prev):
    slopes_a, slopes_b = _alibi_slopes()
    n = bs * ts
    q_a, kv_a, q_b, kv_b, cqkv, cg, ba = _inproj(x2d, lw["gm"], lw["w_mix"], F32)
    rows = page * 2 * HA
    out_a = _diff_attn_sample(q_a.reshape(bs, ts, -1), kv_a.reshape(bs, ts, -1),
                              cache_a.reshape(cache_a.shape[0], cache_a.shape[1], rows, DA), pt_flat,
                              lw["dl"], lw["dgain"], slopes_a, lam_init, layer, n_pages, page)
    out_b = _moba_attn_sample(q_b.reshape(bs, ts, -1), kv_b.reshape(bs, ts, -1),
                              cache_b.reshape(cache_b.shape[0], cache_b.shape[1], rows, DB), pt_flat,
                              slopes_b, layer, n_pages, page)
    tpad = GDN_CHUNK

    def pad_t(a):
        return jnp.pad(a.reshape(bs, ts, -1), ((0, 0), (0, tpad - ts), (0, 0))).reshape(bs * tpad, -1)

    prev8 = jnp.pad(conv_prev.astype(F32), ((0, 0), (SUBLANES - (CONV_W - 1), 0), (0, 0)))
    out_c, s_new = _gdn(pad_t(cqkv), pad_t(cg), pad_t(ba), prev8, s_prev.astype(F32), lw["conv_w"], lw["alog"],
                        lw["dtb"], lw["ggain"], bs, tpad, ts)
    out_c = out_c.reshape(bs, tpad, -1)[:, :ts].reshape(n, -1)
    x1, h2, route, counts = _merge(x2d, out_a.reshape(n, -1).astype(BF16), out_b.reshape(n, -1).astype(BF16), out_c,
                                   lw["gm"], lw["w_gate"], lw["w_branch"], lw["w_out"], lw["gf"], lw["w_router"])
    c_in = jnp.concatenate([conv_prev.astype(F32), cqkv.reshape(bs, ts, C_CONV)], axis=1)
    return (x1, _moe_dispatch(h2, route, counts)), kv_a, kv_b, s_new, c_in[:, ts:]


def kernel(x_prompt, x_sample, cache_diff_kv, cache_moba_kv, state_gdn, state_conv, page_table, norm_mixer, w_in, diff_lambda, diff_norm_gain, gdn_conv_w, gdn_a_log, gdn_dt_bias, gdn_norm_gain, w_branch, w_out, norm_ffn, router_group, router_expert, expert_w_gate, expert_w_up, expert_w_down, norm_final):
    bp, tp, d = x_prompt.shape
    bs, ts, _ = x_sample.shape
    depth = w_in.shape[0]
    n_pages = page_table.shape[1]
    page = cache_diff_kv.shape[2]
    pt_flat = page_table.reshape(-1).astype(jnp.int32)
    final_gain = norm_final.reshape(1, d)

    xp = x_prompt.reshape(bp * tp, d)
    xs = x_sample.reshape(bs * ts, d)
    outs_p = [[], [], [], []]
    outs_s = [[], [], [], []]
    for l in range(depth):
        lw = _layer_weights(l, w_in, norm_mixer, diff_lambda, diff_norm_gain, gdn_conv_w, gdn_a_log, gdn_dt_bias,
                            gdn_norm_gain, w_branch, w_out, norm_ffn, router_group, router_expert, expert_w_gate,
                            expert_w_up, expert_w_down)
        lam_init = 0.8 - 0.6 * math.exp(-0.3 * l)
        fg = final_gain if l == depth - 1 else None
        pend_p, kva, kvb, sg, sc = _prompt_layer(xp, lw, lam_init, bp, tp)
        for acc, v in zip(outs_p, (kva.reshape(bp, tp, 2, HA, DA), kvb.reshape(bp, tp, 2, HB, DB), sg, sc)):
            acc.append(v)
        pend_s, kva, kvb, sg, sc = _sample_layer(xs, lw, lam_init, bs, ts, l, cache_diff_kv, cache_moba_kv, pt_flat,
                                                 n_pages, page, state_conv[l], state_gdn[l])
        for acc, v in zip(outs_s, (kva.reshape(bs, ts, 2, HA, DA), kvb.reshape(bs, ts, 2, HB, DB),
                                   sg.astype(state_gdn.dtype), sc)):
            acc.append(v)
        xp = _layer_finish(pend_p, lw, fg)
        xs = _layer_finish(pend_s, lw, fg)
    return (xp.reshape(bp, tp, d), xs.reshape(bs, ts, d),
            jnp.stack(outs_p[0]), jnp.stack(outs_p[1]), jnp.stack(outs_p[2]), jnp.stack(outs_p[3]),
            jnp.stack(outs_s[0]), jnp.stack(outs_s[1]), jnp.stack(outs_s[2]), jnp.stack(outs_s[3]))
```
